```python
import jax, jax.numpy as jnp
from jax import lax
import numpy as np

D_MODEL = 1024
BATCH = 8
SEQ = 4096
DEPTH = 1

N_MEM = 256
GRID_W = 64
HEAD_DIM = 128
D_MIX = 2 * D_MODEL
A_WIDTH = D_MIX // 2
A_HEADS = A_WIDTH // HEAD_DIM
A_KV_HEADS = A_HEADS // 4
A_GROUP = A_HEADS // A_KV_HEADS
A_KV_WIDTH = A_KV_HEADS * HEAD_DIM
Q_BLOCK = 128
ROPE_AXIS_DIM = HEAD_DIM // 2
ROPE_THETA = 10000.0
B_WIDTH = D_MIX // 4
B_HEADS = B_WIDTH // HEAD_DIM
B_QKV_WIDTH = 3 * B_WIDTH
N_DIR = 2
CONV_K = 5
CHUNK = 64
M_WIDTH = D_MIX // 4
M_HEADS = M_WIDTH // HEAD_DIM
IN_WIDTHS = (A_WIDTH, A_KV_WIDTH, A_KV_WIDTH, B_QKV_WIDTH, N_DIR * B_HEADS, N_DIR * B_HEADS, M_WIDTH, D_MIX)
IN_WIDTH = A_WIDTH + 2 * A_KV_WIDTH + B_QKV_WIDTH + 2 * N_DIR * B_HEADS + M_WIDTH + D_MIX
EPS = 1e-6

kernel_name = "hybrid_parallel_axialgqa_gdeltanet_memxattn"


def rms_norm(x, w):
    x32 = x.astype(jnp.float32)
    y = x32 * lax.rsqrt(jnp.mean(x32 * x32, axis=-1, keepdims=True) + EPS)
    return (y * w.astype(jnp.float32)).astype(x.dtype)


def l2_norm(x):
    x32 = x.astype(jnp.float32)
    return x32 * lax.rsqrt(jnp.sum(x32 * x32, axis=-1, keepdims=True) + EPS)


def split_points():
    pts, acc = [], 0
    for w in IN_WIDTHS[:-1]:
        acc += w
        pts.append(acc)
    return pts


def axial_rope_angles(seq_len):
    rows = seq_len // GRID_W
    row_id = jnp.repeat(jnp.arange(rows), GRID_W)
    col_id = jnp.arange(seq_len) % GRID_W
    inv_freq = ROPE_THETA ** (-jnp.arange(0, ROPE_AXIS_DIM, 2, dtype=jnp.float32) / ROPE_AXIS_DIM)
    ang_row = row_id.astype(jnp.float32)[:, None] * inv_freq
    ang_col = col_id.astype(jnp.float32)[:, None] * inv_freq
    return ang_row, ang_col


def rope_rotate(x, ang):
    cos = jnp.cos(ang)[:, None, :]
    sin = jnp.sin(ang)[:, None, :]
    x1, x2 = jnp.split(x, 2, axis=-1)
    return jnp.concatenate([x1 * cos - x2 * sin, x2 * cos + x1 * sin], axis=-1)


def apply_axial_rope(x, ang_row, ang_col):
    xr, xc = jnp.split(x, 2, axis=-1)
    return jnp.concatenate([rope_rotate(xr, ang_row), rope_rotate(xc, ang_col)], axis=-1).astype(x.dtype)


def axial_gqa_attention(q, k, v):
    b, s = q.shape[:2]
    nblk = s // Q_BLOCK
    qb = (q * HEAD_DIM ** -0.5).reshape(b, nblk, Q_BLOCK, A_KV_HEADS, A_GROUP, HEAD_DIM)
    qb = qb.transpose(1, 0, 3, 4, 2, 5)
    kt = k.transpose(0, 2, 1, 3)
    vt = v.transpose(0, 2, 1, 3)

    def block(qblk):
        sc = jnp.einsum('bkgqd,bksd->bkgqs', qblk, kt).astype(jnp.float32)
        p = jax.nn.softmax(sc, axis=-1).astype(vt.dtype)
        return jnp.einsum('bkgqs,bksd->bkgqd', p, vt)

    o = lax.map(block, qb)
    return o.transpose(1, 0, 4, 2, 3, 5).reshape(b, s, A_WIDTH)


def centred_short_conv(x, w):
    c = x.shape[-1]
    y = lax.conv_general_dilated(
        x, w[:, None, :].astype(x.dtype), window_strides=(1,),
        padding=[(CONV_K // 2, CONV_K // 2)],
        dimension_numbers=('NWC', 'WIO', 'NWC'), feature_group_count=c)
    return jax.nn.silu(y)


def gated_delta_rule_chunked(q, k, v, g, beta):
    *lead, seq_len, dk = q.shape
    dv = v.shape[-1]
    n = seq_len // CHUNK
    nl = len(lead)
    q = (q * dk ** -0.5).reshape(*lead, n, CHUNK, dk)
    k = k.reshape(*lead, n, CHUNK, dk)
    v = v.reshape(*lead, n, CHUNK, dv)
    beta = beta.reshape(*lead, n, CHUNK)
    g = jnp.cumsum(g.reshape(*lead, n, CHUNK), axis=-1)
    incl = jnp.tril(jnp.ones((CHUNK, CHUNK), dtype=bool))
    strict = jnp.tril(jnp.ones((CHUNK, CHUNK), dtype=bool), k=-1)
    decay = jnp.exp(jnp.where(incl, g[..., :, None] - g[..., None, :], -jnp.inf))
    k_beta = k * beta[..., None]
    lower = jnp.where(strict, jnp.einsum('...id,...jd->...ij', k_beta, k) * decay, 0.0)
    t_mat = lower + jnp.eye(CHUNK, dtype=lower.dtype)
    rhs = jnp.concatenate([v * beta[..., None], k_beta * jnp.exp(g)[..., None]], axis=-1)
    sol = lax.linalg.triangular_solve(t_mat, rhs, left_side=True, lower=True, unit_diagonal=True)
    u, w = sol[..., :dv], sol[..., dv:]
    intra = jnp.einsum('...id,...jd->...ij', q, k) * decay
    k_tail = k * jnp.exp(g[..., -1:] - g)[..., None]
    q_head = q * jnp.exp(g)[..., None]
    chunk_decay = jnp.exp(g[..., -1])
    xs = tuple(jnp.moveaxis(t, nl, 0) for t in (q_head, k_tail, u, w, intra, chunk_decay))

    def step(state, inp):
        qc, kc, uc, wc, ac, dc = inp
        v_new = uc - jnp.einsum('...cd,...de->...ce', wc, state)
        o = jnp.einsum('...cd,...de->...ce', qc, state) + jnp.einsum('...ij,...je->...ie', ac, v_new)
        state = state * dc[..., None, None] + jnp.einsum('...cd,...ce->...de', kc, v_new)
        return state, o

    state0 = jnp.zeros((*lead, dk, dv), jnp.float32)
    _, o = lax.scan(step, state0, xs)
    return jnp.moveaxis(o, 0, nl).reshape(*lead, seq_len, dv)


def bidir_gated_deltanet(q, k, v, a, b, a_log, dt_bias):
    g = -jnp.exp(a_log.astype(jnp.float32)) * jax.nn.softplus(a.astype(jnp.float32) + dt_bias.astype(jnp.float32))
    beta = jax.nn.sigmoid(b.astype(jnp.float32))
    g_t = g.transpose(2, 0, 3, 1)
    beta_t = beta.transpose(2, 0, 3, 1)
    g_dir = jnp.stack([g_t[0], jnp.flip(g_t[1], axis=-1)])
    beta_dir = jnp.stack([beta_t[0], jnp.flip(beta_t[1], axis=-1)])

    def both(t):
        t = t.transpose(0, 2, 1, 3)
        return jnp.stack([t, jnp.flip(t, axis=2)])

    o = gated_delta_rule_chunked(both(l2_norm(q)), both(l2_norm(k)), both(v.astype(jnp.float32)), g_dir, beta_dir)
    o = o[0] + jnp.flip(o[1], axis=2)
    return o.transpose(0, 2, 1, 3)


def memory_cross_attention(q, k, v):
    b, s = q.shape[:2]
    sc = jnp.einsum('bshd,bmhd->bhsm', q * HEAD_DIM ** -0.5, k).astype(jnp.float32)
    p = jax.nn.softmax(sc, axis=-1).astype(v.dtype)
    return jnp.einsum('bhsm,bmhd->bshd', p, v).reshape(b, s, M_WIDTH)


def setup_inputs(seed: int = 0) -> dict:
    key = jax.random.key(seed)
    ks = jax.random.split(key, 16)
    f32 = jnp.float32
    nrm = lambda k, shp: jax.random.normal(k, shp, f32)
    dt = jnp.exp(jax.random.uniform(ks[8], (DEPTH, N_DIR, B_HEADS), f32, np.log(1e-3), np.log(1e-1)))
    return {
        "x": nrm(ks[0], (BATCH, SEQ, D_MODEL)),
        "mem": nrm(ks[1], (BATCH, N_MEM, D_MODEL)),
        "norm_pre_w": 1.0 + 0.02 * nrm(ks[2], (DEPTH, D_MODEL)),
        "w_in": nrm(ks[3], (DEPTH, D_MODEL, IN_WIDTH)) * D_MODEL ** -0.5,
        "q_norm_w": 1.0 + 0.02 * nrm(ks[4], (DEPTH, HEAD_DIM)),
        "k_norm_w": 1.0 + 0.02 * nrm(ks[5], (DEPTH, HEAD_DIM)),
        "conv_w": nrm(ks[6], (DEPTH, CONV_K, B_QKV_WIDTH)) * CONV_K ** -0.5,
        "a_log": jnp.log(jax.random.uniform(ks[7], (DEPTH, N_DIR, B_HEADS), f32, 1.0, 16.0)),
        "dt_bias": dt + jnp.log(-jnp.expm1(-dt)),
        "delta_norm_w": 1.0 + 0.02 * nrm(ks[9], (DEPTH, HEAD_DIM)),
        "mem_norm_w": 1.0 + 0.02 * nrm(ks[10], (DEPTH, D_MODEL)),
        "w_mem_kv": nrm(ks[11], (DEPTH, D_MODEL, 2 * M_WIDTH)) * D_MODEL ** -0.5,
        "w_out": nrm(ks[12], (DEPTH, D_MIX, D_MODEL)) * D_MIX ** -0.5,
        "norm_post_w": 1.0 + 0.02 * nrm(ks[13], (DEPTH, D_MODEL)),
    }


def reference(x, mem, norm_pre_w, w_in, q_norm_w, k_norm_w, conv_w, a_log, dt_bias,
              delta_norm_w, mem_norm_w, w_mem_kv, w_out, norm_post_w):
    b, s, _ = x.shape
    n_mem = mem.shape[1]
    ang_row, ang_col = axial_rope_angles(s)
    pts = split_points()
    for l in range(DEPTH):
        h = rms_norm(x, norm_pre_w[l])
        proj = h @ w_in[l]
        aq, ak, av, bqkv, ba, bb, mq, z = jnp.split(proj, pts, axis=-1)

        aq = apply_axial_rope(rms_norm(aq.reshape(b, s, A_HEADS, HEAD_DIM), q_norm_w[l]), ang_row, ang_col)
        ak = apply_axial_rope(rms_norm(ak.reshape(b, s, A_KV_HEADS, HEAD_DIM), k_norm_w[l]), ang_row, ang_col)
        av = av.reshape(b, s, A_KV_HEADS, HEAD_DIM)
        y_a = axial_gqa_attention(aq, ak, av)

        bqkv = centred_short_conv(bqkv, conv_w[l])
        bq, bk, bv = (t.reshape(b, s, B_HEADS, HEAD_DIM) for t in jnp.split(bqkv, 3, axis=-1))
        y_b = bidir_gated_deltanet(bq, bk, bv, ba.reshape(b, s, N_DIR, B_HEADS),
                                   bb.reshape(b, s, N_DIR, B_HEADS), a_log[l], dt_bias[l])
        y_b = rms_norm(y_b, delta_norm_w[l]).reshape(b, s, B_WIDTH).astype(x.dtype)

        mkv = rms_norm(mem, mem_norm_w[l]) @ w_mem_kv[l]
        mk, mv = (t.reshape(b, n_mem, M_HEADS, HEAD_DIM) for t in jnp.split(mkv, 2, axis=-1))
        y_m = memory_cross_attention(mq.reshape(b, s, M_HEADS, HEAD_DIM), mk, mv)

        mix = jnp.concatenate([y_a, y_b, y_m], axis=-1) * jax.nn.silu(z)
        x = x + rms_norm(mix @ w_out[l], norm_post_w[l])
    return x
```

```python
import functools

import jax
import jax.numpy as jnp
import numpy as np
from jax import lax
from jax.experimental import pallas as pl
from jax.experimental.pallas import tpu as pltpu

F32 = jnp.float32
BF16 = jnp.bfloat16

D_MODEL = 1024
HEAD_DIM = 128
GRID_W = 64
A_HEADS = 8
A_KV_HEADS = 2
A_GROUP = A_HEADS // A_KV_HEADS
A_WIDTH = A_HEADS * HEAD_DIM
A_KV_WIDTH = A_KV_HEADS * HEAD_DIM
B_HEADS = 4
B_WIDTH = B_HEADS * HEAD_DIM
B_QKV_WIDTH = 3 * B_WIDTH
N_DIR = 2
CONV_K = 5
CHUNK = 64
M_HEADS = 4
M_WIDTH = M_HEADS * HEAD_DIM
D_MIX = A_WIDTH + B_WIDTH + M_WIDTH
ROPE_AXIS_DIM = HEAD_DIM // 2
ROPE_THETA = 10000.0
EPS = 1e-6
GATE_WIDTH = N_DIR * HEAD_DIM
QK_WIDTH = A_WIDTH + A_KV_WIDTH
SUBLANES = 8
VMEM_LIMIT = 56 * 1024 * 1024

_NT = (((1,), (1,)), ((), ()))
_TN = (((0,), (0,)), ((), ()))


def _dot(a, b, dims=None, precision=None):
    if dims is None:
        return jnp.dot(a, b, preferred_element_type=F32, precision=precision)
    return lax.dot_general(a, b, dims, preferred_element_type=F32, precision=precision)


def _silu(x):
    return x * jax.nn.sigmoid(x)


def _params(sem):
    return pltpu.CompilerParams(dimension_semantics=sem, vmem_limit_bytes=VMEM_LIMIT)


def _in_proj_body(x_ref, nw_ref, w_ref, qkw_ref, cos_ref, sin_ref,
                  aq_ref, ak_ref, av_ref, bqkv_ref, gb_ref, mq_ref, z_ref):
    x = x_ref[...]
    ms = jnp.mean(x * x, axis=-1, keepdims=True)
    h = (x * lax.rsqrt(ms + EPS) * nw_ref[...]).astype(BF16)

    cos = cos_ref[...]
    sin = sin_ref[...]
    lane = lax.broadcasted_iota(jnp.int32, cos.shape, 1)
    first_half = (lane % (ROPE_AXIS_DIM)) < (ROPE_AXIS_DIM // 2)

    def norm_rope(y, w):
        yn = y * lax.rsqrt(jnp.mean(y * y, axis=-1, keepdims=True) + EPS) * w
        swapped = jnp.where(first_half,
                            pltpu.roll(yn, HEAD_DIM - ROPE_AXIS_DIM // 2, 1),
                            pltpu.roll(yn, ROPE_AXIS_DIM // 2, 1))
        return yn * cos + swapped * sin

    col = 0
    for c0 in range(0, QK_WIDTH, 2 * HEAD_DIM):
        y = _dot(h, w_ref[:, c0:c0 + 2 * HEAD_DIM])
        for j in range(2):
            hc = c0 + j * HEAD_DIM
            yh = y[:, j * HEAD_DIM:(j + 1) * HEAD_DIM]
            if hc < A_WIDTH:
                out = norm_rope(yh, qkw_ref[0:1, :]) * (HEAD_DIM ** -0.5)
                aq_ref[:, hc:hc + HEAD_DIM] = out.astype(BF16)
            else:
                out = norm_rope(yh, qkw_ref[1:2, :])
                ak_ref[:, hc - A_WIDTH:hc - A_WIDTH + HEAD_DIM] = out.astype(BF16)
    col = QK_WIDTH
    av_ref[...] = _dot(h, w_ref[:, col:col + A_KV_WIDTH]).astype(BF16)
    col += A_KV_WIDTH
    for c0 in range(0, B_QKV_WIDTH, 512):
        bqkv_ref[:, c0:c0 + 512] = _dot(h, w_ref[:, col + c0:col + c0 + 512])
    col += B_QKV_WIDTH
    gb_ref[...] = _dot(h, w_ref[:, col:col + GATE_WIDTH])
    col += GATE_WIDTH
    mq_ref[...] = (_dot(h, w_ref[:, col:col + M_WIDTH]) * (HEAD_DIM ** -0.5)).astype(BF16)
    col += M_WIDTH
    for c0 in range(0, D_MIX, 512):
        z_ref[:, c0:c0 + 512] = _dot(h, w_ref[:, col + c0:col + c0 + 512]).astype(BF16)


def _in_proj(x2, nw, w_all, qkw, cos_t, sin_t, seq, tm):
    rows = x2.shape[0]
    n_seq_tiles = seq // tm
    wtot = w_all.shape[1]
    row = lambda i: (i, 0)
    const = lambda i: (0, 0)
    outs = [
        (A_WIDTH, BF16), (A_KV_WIDTH, BF16), (A_KV_WIDTH, BF16), (B_QKV_WIDTH, F32),
        (GATE_WIDTH, F32), (M_WIDTH, BF16), (D_MIX, BF16),
    ]
    return pl.pallas_call(
        _in_proj_body,
        grid=(rows // tm,),
        in_specs=[
            pl.BlockSpec((tm, D_MODEL), row),
            pl.BlockSpec((1, D_MODEL), const),
            pl.BlockSpec((D_MODEL, wtot), const, pipeline_mode=pl.Buffered(1)),
            pl.BlockSpec((SUBLANES, HEAD_DIM), const),
            pl.BlockSpec((tm, HEAD_DIM), lambda i: (i % n_seq_tiles, 0)),
            pl.BlockSpec((tm, HEAD_DIM), lambda i: (i % n_seq_tiles, 0)),
        ],
        out_specs=[pl.BlockSpec((tm, w), row) for w, _ in outs],
        out_shape=[jax.ShapeDtypeStruct((rows, w), dt) for w, dt in outs],
        compiler_params=_params(("parallel",)),
        name="in_proj",
    )(x2, nw, w_all, qkw, cos_t, sin_t)


def _attn_body(q_ref, k_ref, v_ref, z_ref, o_ref, *, tq, tk, seq):
    q = jnp.concatenate([q_ref[:, g * HEAD_DIM:(g + 1) * HEAD_DIM] for g in range(A_GROUP)], axis=0)
    rows = A_GROUP * tq

    def step(j, carry):
        m, l, acc = carry
        start = pl.multiple_of(j * tk, tk)
        kt = k_ref[pl.ds(start, tk), :]
        vt = v_ref[pl.ds(start, tk), :]
        s = _dot(q, kt, _NT)
        m_new = jnp.maximum(m, jnp.max(s, axis=-1, keepdims=True))
        alpha = jnp.exp(m - m_new)
        p = jnp.exp(s - m_new)
        l = alpha * l + jnp.sum(p, axis=-1, keepdims=True)
        acc = alpha * acc + _dot(p.astype(BF16), vt)
        return m_new, l, acc

    init = (jnp.full((rows, 1), -jnp.inf, F32), jnp.zeros((rows, 1), F32), jnp.zeros((rows, HEAD_DIM), F32))
    _, l, acc = lax.fori_loop(0, seq // tk, step, init)
    y = acc / l
    for g in range(A_GROUP):
        zg = z_ref[:, g * HEAD_DIM:(g + 1) * HEAD_DIM].astype(F32)
        o_ref[:, g * HEAD_DIM:(g + 1) * HEAD_DIM] = (y[g * tq:(g + 1) * tq] * _silu(zg)).astype(BF16)


def _attention(aq, ak, av, z, tq, tk):
    b, seq, _ = aq.shape
    gw = A_GROUP * HEAD_DIM
    return pl.pallas_call(
        functools.partial(_attn_body, tq=tq, tk=tk, seq=seq),
        grid=(b, A_KV_HEADS, seq // tq),
        in_specs=[
            pl.BlockSpec((None, tq, gw), lambda bi, kv, i: (bi, i, kv)),
            pl.BlockSpec((None, seq, HEAD_DIM), lambda bi, kv, i: (bi, 0, kv)),
            pl.BlockSpec((None, seq, HEAD_DIM), lambda bi, kv, i: (bi, 0, kv)),
            pl.BlockSpec((None, tq, gw), lambda bi, kv, i: (bi, i, kv)),
        ],
        out_specs=pl.BlockSpec((None, tq, gw), lambda bi, kv, i: (bi, i, kv)),
        out_shape=jax.ShapeDtypeStruct((b, seq, A_WIDTH), BF16),
        compiler_params=_params(("parallel", "parallel", "arbitrary")),
        name="attn",
    )(aq, ak, av, z)


def _gdn_prep_body(cur_ref, prev_ref, next_ref, cw_ref, gb_ref, alog_ref, dtb_ref,
                   qkv_ref, gate_ref, ext_ref, *, tm, n_tiles):
    i = pl.program_id(1)
    halo = SUBLANES
    prev = prev_ref[...]
    nxt = next_ref[...]
    ext_ref[0:halo, :] = jnp.where(i == 0, jnp.zeros_like(prev), prev)
    ext_ref[halo:halo + tm, :] = cur_ref[...]
    ext_ref[halo + tm:, :] = jnp.where(i == n_tiles - 1, jnp.zeros_like(nxt), nxt)

    y = None
    for j in range(CONV_K):
        term = ext_ref[pl.ds(halo - CONV_K // 2 + j, tm), :] * cw_ref[j:j + 1, :]
        y = term if y is None else y + term
    y = _silu(y)

    for hh in range(2 * B_HEADS):
        c0 = hh * HEAD_DIM
        yh = y[:, c0:c0 + HEAD_DIM]
        yn = yh * lax.rsqrt(jnp.sum(yh * yh, axis=-1, keepdims=True) + EPS)
        if hh < B_HEADS:
            yn = yn * (HEAD_DIM ** -0.5)
        qkv_ref[:, c0:c0 + HEAD_DIM] = yn.astype(BF16)
    qkv_ref[:, 2 * B_WIDTH:] = y[:, 2 * B_WIDTH:].astype(BF16)

    g = gb_ref[...]
    lane = lax.broadcasted_iota(jnp.int32, g.shape, 1) % HEAD_DIM
    t = g + dtb_ref[...]
    softplus = jnp.maximum(t, 0.0) + jnp.log1p(jnp.exp(-jnp.abs(t)))
    glog = -jnp.exp(alog_ref[...]) * softplus
    beta = jax.nn.sigmoid(g)
    gate_ref[...] = jnp.where(lane < B_HEADS, glog, jnp.where(lane < 2 * B_HEADS, beta, 0.0))


def _gdn_prep(bqkv, cw, gb, alog_v, dtb_v, tm):
    b, seq, _ = bqkv.shape
    n_tiles = seq // tm
    per = tm // SUBLANES
    last_blk = seq // SUBLANES - 1
    return pl.pallas_call(
        functools.partial(_gdn_prep_body, tm=tm, n_tiles=n_tiles),
        grid=(b, n_tiles),
        in_specs=[
            pl.BlockSpec((None, tm, B_QKV_WIDTH), lambda bi, i: (bi, i, 0)),
            pl.BlockSpec((None, SUBLANES, B_QKV_WIDTH), lambda bi, i: (bi, jnp.maximum(i * per - 1, 0), 0)),
            pl.BlockSpec((None, SUBLANES, B_QKV_WIDTH), lambda bi, i: (bi, jnp.minimum((i + 1) * per, last_blk), 0)),
            pl.BlockSpec((SUBLANES, B_QKV_WIDTH), lambda bi, i: (0, 0)),
            pl.BlockSpec((None, tm, GATE_WIDTH), lambda bi, i: (bi, i, 0)),
            pl.BlockSpec((1, GATE_WIDTH), lambda bi, i: (0, 0)),
            pl.BlockSpec((1, GATE_WIDTH), lambda bi, i: (0, 0)),
        ],
        out_specs=[
            pl.BlockSpec((None, tm, B_QKV_WIDTH), lambda bi, i: (bi, i, 0)),
            pl.BlockSpec((None, tm, GATE_WIDTH), lambda bi, i: (bi, i, 0)),
        ],
        out_shape=[
            jax.ShapeDtypeStruct((b, seq, B_QKV_WIDTH), BF16),
            jax.ShapeDtypeStruct((b, seq, GATE_WIDTH), F32),
        ],
        scratch_shapes=[pltpu.VMEM((tm + 2 * SUBLANES, B_QKV_WIDTH), F32)],
        compiler_params=_params(("parallel", "parallel")),
        name="gdn_prep",
    )(bqkv, bqkv, bqkv, cw, gb, alog_v, dtb_v)


def _gdn_scan_body(q_ref, k_ref, v_ref, g_ref, o_ref, s_ref):
    d = pl.program_id(1)
    c = pl.program_id(2)

    @pl.when(c == 0)
    def _():
        s_ref[...] = jnp.zeros_like(s_ref)

    sgn = jnp.where(d == 0, 1, -1)
    r = lax.broadcasted_iota(jnp.int32, (CHUNK, CHUNK), 0)
    cc = lax.broadcasted_iota(jnp.int32, (CHUNK, CHUNK), 1)
    diff = (r - cc) * sgn
    incl = diff >= 0
    strict = diff > 0
    m_incl = incl.astype(F32)
    m_strict = strict.astype(F32)
    eye = (r == cc).astype(F32)

    gt = g_ref[...]
    hi = lax.Precision.HIGHEST
    gcum = _dot(m_incl, gt, precision=hi)
    gtot = jnp.sum(gt, axis=0, keepdims=True)

    for h in range(B_HEADS):
        hs = slice(h * HEAD_DIM, (h + 1) * HEAD_DIM)
        gl = gt[:, h:h + 1]
        g = gcum[:, h:h + 1]
        beta = gt[:, B_HEADS + h:B_HEADS + h + 1]
        g_last = gtot[:, h:h + 1]
        dm = _dot(m_incl, gl * m_strict, precision=hi)
        decay = jnp.exp(dm)
        kh = k_ref[:, hs]
        qh = q_ref[:, hs]
        vh = v_ref[:, hs]
        kq = _dot(jnp.concatenate([kh, qh], axis=0), kh, _NT)
        kk = kq[:CHUNK]
        qk = kq[CHUNK:]
        x = -jnp.where(strict, kk * beta * decay, 0.0)
        p = eye + x
        xp = x
        for _ in range(5):
            xb = xp.astype(BF16)
            xp = _dot(xb, xb)
            p = p + _dot(p.astype(BF16), xp.astype(BF16))
        eg = jnp.exp(g)
        kf = kh.astype(F32)
        rhs = jnp.concatenate([vh.astype(F32) * beta, kf * (beta * eg)], axis=1).astype(BF16)
        uw = _dot(p.astype(BF16), rhs)
        u = uw[:, :HEAD_DIM]
        w = uw[:, HEAD_DIM:]
        state = s_ref[h]
        wq = jnp.concatenate([w, qh.astype(F32) * eg], axis=0).astype(BF16)
        ws = _dot(wq, state.astype(BF16))
        v_new = (u - ws[:CHUNK]).astype(BF16)
        intra = jnp.where(incl, qk * decay, 0.0).astype(BF16)
        o_ref[:, hs] = ws[CHUNK:] + _dot(intra, v_new)
        k_tail = (kf * jnp.exp(g_last - g)).astype(BF16)
        s_ref[h] = state * jnp.exp(g_last) + _dot(k_tail, v_new, _TN)


def _gdn_scan(qkvn, gates):
    b, seq, _ = qkvn.shape
    nc = seq // CHUNK

    def chunk(d, c):
        return c + d * (nc - 1 - 2 * c)

    return pl.pallas_call(
        _gdn_scan_body,
        grid=(b, N_DIR, nc),
        in_specs=[
            pl.BlockSpec((None, CHUNK, B_WIDTH), lambda bi, d, c: (bi, chunk(d, c), 0)),
            pl.BlockSpec((None, CHUNK, B_WIDTH), lambda bi, d, c: (bi, chunk(d, c), 1)),
            pl.BlockSpec((None, CHUNK, B_WIDTH), lambda bi, d, c: (bi, chunk(d, c), 2)),
            pl.BlockSpec((None, CHUNK, HEAD_DIM), lambda bi, d, c: (bi, chunk(d, c), d)),
        ],
        out_specs=pl.BlockSpec((None, None, CHUNK, B_WIDTH), lambda bi, d, c: (bi, d, chunk(d, c), 0)),
        out_shape=jax.ShapeDtypeStruct((b, N_DIR, seq, B_WIDTH), F32),
        scratch_shapes=[pltpu.VMEM((B_HEADS, HEAD_DIM, HEAD_DIM), F32)],
        compiler_params=_params(("parallel", "parallel", "arbitrary")),
        name="gdn_scan",
    )(qkvn, qkvn, qkvn, gates)


def _mem_kv_body(m_ref, nw_ref, w_ref, o_ref):
    x = m_ref[...]
    ms = jnp.mean(x * x, axis=-1, keepdims=True)
    h = (x * lax.rsqrt(ms + EPS) * nw_ref[...]).astype(BF16)
    o_ref[...] = _dot(h, w_ref[...]).astype(BF16)


def _mem_kv(mem2, nw, w, tm):
    rows = mem2.shape[0]
    return pl.pallas_call(
        _mem_kv_body,
        grid=(rows // tm,),
        in_specs=[
            pl.BlockSpec((tm, D_MODEL), lambda i: (i, 0)),
            pl.BlockSpec((1, D_MODEL), lambda i: (0, 0)),
            pl.BlockSpec((D_MODEL, 2 * M_WIDTH), lambda i: (0, 0)),
        ],
        out_specs=pl.BlockSpec((tm, 2 * M_WIDTH), lambda i: (i, 0)),
        out_shape=jax.ShapeDtypeStruct((rows, 2 * M_WIDTH), BF16),
        compiler_params=_params(("parallel",)),
        name="mem_kv",
    )(mem2, nw, w)


def _mem_attn_body(q_ref, kv_ref, z_ref, o_ref):
    for h in range(M_HEADS):
        hs = slice(h * HEAD_DIM, (h + 1) * HEAD_DIM)
        s = _dot(q_ref[:, hs], kv_ref[:, hs], _NT)
        m = jnp.max(s, axis=-1, keepdims=True)
        p = jnp.exp(s - m)
        l = jnp.sum(p, axis=-1, keepdims=True)
        vh = kv_ref[:, M_WIDTH + h * HEAD_DIM:M_WIDTH + (h + 1) * HEAD_DIM]
        y = _dot(p.astype(BF16), vh) / l
        o_ref[:, hs] = (y * _silu(z_ref[:, hs].astype(F32))).astype(BF16)


def _mem_attn(mq, mkv, z, tq):
    b, seq, _ = mq.shape
    n_mem = mkv.shape[1]
    z_blk = (A_WIDTH + B_WIDTH) // M_WIDTH
    return pl.pallas_call(
        _mem_attn_body,
        grid=(b, seq // tq),
        in_specs=[
            pl.BlockSpec((None, tq, M_WIDTH), lambda bi, i: (bi, i, 0)),
            pl.BlockSpec((None, n_mem, 2 * M_WIDTH), lambda bi, i: (bi, 0, 0)),
            pl.BlockSpec((None, tq, M_WIDTH), lambda bi, i: (bi, i, z_blk)),
        ],
        out_specs=pl.BlockSpec((None, tq, M_WIDTH), lambda bi, i: (bi, i, 0)),
        out_shape=jax.ShapeDtypeStruct((b, seq, M_WIDTH), BF16),
        compiler_params=_params(("parallel", "parallel")),
        name="mem_attn",
    )(mq, mkv, z)


def _out_proj_body(ya_ref, of_ref, ob_ref, zb_ref, ym_ref, dnw_ref, w_ref, pw_ref, x_ref, o_ref):
    ob = of_ref[...] + ob_ref[...]
    dnw = dnw_ref[...]
    parts = []
    for h in range(B_HEADS):
        hs = slice(h * HEAD_DIM, (h + 1) * HEAD_DIM)
        oh = ob[:, hs]
        yn = oh * lax.rsqrt(jnp.mean(oh * oh, axis=-1, keepdims=True) + EPS) * dnw
        parts.append((yn * _silu(zb_ref[:, hs].astype(F32))).astype(BF16))
    yb = jnp.concatenate(parts, axis=1)
    acc = _dot(ya_ref[...], w_ref[0:A_WIDTH, :])
    acc = acc + _dot(yb, w_ref[A_WIDTH:A_WIDTH + B_WIDTH, :])
    acc = acc + _dot(ym_ref[...], w_ref[A_WIDTH + B_WIDTH:, :])
    y = acc * lax.rsqrt(jnp.mean(acc * acc, axis=-1, keepdims=True) + EPS) * pw_ref[...]
    o_ref[...] = x_ref[...] + y


def _out_proj(ya, o_dir, z, ym, dnw, w, pw, x, tm):
    b, seq, _ = ya.shape
    z_blk = A_WIDTH // B_WIDTH
    row = lambda bi, i: (bi, i, 0)
    const = lambda bi, i: (0, 0)
    return pl.pallas_call(
        _out_proj_body,
        grid=(b, seq // tm),
        in_specs=[
            pl.BlockSpec((None, tm, A_WIDTH), row),
            pl.BlockSpec((None, None, tm, B_WIDTH), lambda bi, i: (bi, 0, i, 0)),
            pl.BlockSpec((None, None, tm, B_WIDTH), lambda bi, i: (bi, 1, i, 0)),
            pl.BlockSpec((None, tm, B_WIDTH), lambda bi, i: (bi, i, z_blk)),
            pl.BlockSpec((None, tm, M_WIDTH), row),
            pl.BlockSpec((1, HEAD_DIM), const),
            pl.BlockSpec((D_MIX, D_MODEL), const),
            pl.BlockSpec((1, D_MODEL), const),
            pl.BlockSpec((None, tm, D_MODEL), row),
        ],
        out_specs=pl.BlockSpec((None, tm, D_MODEL), row),
        out_shape=jax.ShapeDtypeStruct((b, seq, D_MODEL), F32),
        compiler_params=_params(("parallel", "parallel")),
        name="out_proj",
    )(ya, o_dir, o_dir, z, ym, dnw, w, pw, x)


def _rope_tables(seq):
    t = np.arange(seq)
    inv_freq = ROPE_THETA ** (-jnp.arange(0, ROPE_AXIS_DIM, 2, dtype=F32) / ROPE_AXIS_DIM)
    ang_row = jnp.asarray(t // GRID_W, F32)[:, None] * inv_freq
    ang_col = jnp.asarray(t % GRID_W, F32)[:, None] * inv_freq
    cr, sr, cc, sc = jnp.cos(ang_row), jnp.sin(ang_row), jnp.cos(ang_col), jnp.sin(ang_col)
    return (jnp.concatenate([cr, cr, cc, cc], axis=-1),
            jnp.concatenate([-sr, sr, -sc, sc], axis=-1))


def _arrange_w_in(w):
    pts = np.cumsum([A_WIDTH, A_KV_WIDTH, A_KV_WIDTH, B_QKV_WIDTH, N_DIR * B_HEADS, N_DIR * B_HEADS, M_WIDTH])
    aq, ak, av, bqkv, ba, bb, mq, z = jnp.split(w, pts, axis=1)
    gate_blocks = []
    for d in range(N_DIR):
        sl = slice(d * B_HEADS, (d + 1) * B_HEADS)
        pad = jnp.zeros((w.shape[0], HEAD_DIM - 2 * B_HEADS), w.dtype)
        gate_blocks += [ba[:, sl], bb[:, sl], pad]
    return jnp.concatenate([aq, ak, av, bqkv] + gate_blocks + [mq, z], axis=1).astype(BF16)


def _gate_lane_vec(p):
    out = jnp.zeros((N_DIR, HEAD_DIM), F32).at[:, :B_HEADS].set(p.astype(F32))
    return out.reshape(1, GATE_WIDTH)


def kernel(x, mem, norm_pre_w, w_in, q_norm_w, k_norm_w, conv_w, a_log, dt_bias, delta_norm_w,
           mem_norm_w, w_mem_kv, w_out, norm_post_w):
    b, seq, _ = x.shape
    n_mem = mem.shape[1]
    assert seq % 512 == 0 and seq % GRID_W == 0 and w_in.shape[0] == 1
    l = 0
    tm = 512

    cos_t, sin_t = _rope_tables(seq)
    qkw = jnp.zeros((SUBLANES, HEAD_DIM), F32).at[0].set(q_norm_w[l]).at[1].set(k_norm_w[l])
    aq, ak, av, bqkv, gb, mq, z = _in_proj(
        x.reshape(b * seq, D_MODEL), norm_pre_w[l].reshape(1, D_MODEL), _arrange_w_in(w_in[l]),
        qkw, cos_t, sin_t, seq, tm)
    shp = lambda a: a.reshape(b, seq, a.shape[-1])
    aq, ak, av, bqkv, gb, mq, z = map(shp, (aq, ak, av, bqkv, gb, mq, z))

    ya = _attention(aq, ak, av, z, tq=128, tk=512)

    cw = jnp.zeros((SUBLANES, B_QKV_WIDTH), F32).at[:CONV_K].set(conv_w[l])
    qkvn, gates = _gdn_prep(bqkv, cw, gb, _gate_lane_vec(a_log[l]), _gate_lane_vec(dt_bias[l]), tm=256)
    o_dir = _gdn_scan(qkvn, gates)

    mkv = _mem_kv(mem.reshape(b * n_mem, D_MODEL), mem_norm_w[l].reshape(1, D_MODEL),
                  w_mem_kv[l].astype(BF16), tm=256)
    ym = _mem_attn(mq, mkv.reshape(b, n_mem, 2 * M_WIDTH), z, tq=512)

    return _out_proj(ya, o_dir, z, ym, delta_norm_w[l].reshape(1, HEAD_DIM), w_out[l].astype(BF16),
                     norm_post_w[l].reshape(1, D_MODEL), x, tm=256)
```

```python
import functools

import jax
import jax.numpy as jnp
import numpy as np
from jax import lax
from jax.experimental import pallas as pl
from jax.experimental.pallas import tpu as pltpu

F32 = jnp.float32
BF16 = jnp.bfloat16

D_MODEL = 1024
HEAD_DIM = 128
GRID_W = 64
A_HEADS = 8
A_KV_HEADS = 2
A_GROUP = A_HEADS // A_KV_HEADS
A_WIDTH = A_HEADS * HEAD_DIM
A_KV_WIDTH = A_KV_HEADS * HEAD_DIM
B_HEADS = 4
B_WIDTH = B_HEADS * HEAD_DIM
B_QKV_WIDTH = 3 * B_WIDTH
N_DIR = 2
CONV_K = 5
CHUNK = 64
M_HEADS = 4
M_WIDTH = M_HEADS * HEAD_DIM
D_MIX = A_WIDTH + B_WIDTH + M_WIDTH
ROPE_AXIS_DIM = HEAD_DIM // 2
ROPE_THETA = 10000.0
EPS = 1e-6
GATE_WIDTH = N_DIR * HEAD_DIM
QK_WIDTH = A_WIDTH + A_KV_WIDTH
SUBLANES = 8
VMEM_LIMIT = 56 * 1024 * 1024

_NT = (((1,), (1,)), ((), ()))
_TN = (((0,), (0,)), ((), ()))


def _dot(a, b, dims=None, precision=None):
    if dims is None:
        return jnp.dot(a, b, preferred_element_type=F32, precision=precision)
    return lax.dot_general(a, b, dims, preferred_element_type=F32, precision=precision)


def _silu(x):
    return x * jax.nn.sigmoid(x)


def _params(sem):
    return pltpu.CompilerParams(dimension_semantics=sem, vmem_limit_bytes=VMEM_LIMIT)


def _in_proj_body(x_ref, nw_ref, w_ref, qkw_ref, cos_ref, sin_ref,
                  aq_ref, ak_ref, av_ref, bqkv_ref, gb_ref, mq_ref, z_ref):
    x = x_ref[...]
    ms = jnp.mean(x * x, axis=-1, keepdims=True)
    h = (x * lax.rsqrt(ms + EPS) * nw_ref[...]).astype(BF16)

    cos = cos_ref[...]
    sin = sin_ref[...]
    lane = lax.broadcasted_iota(jnp.int32, cos.shape, 1)
    first_half = (lane % (ROPE_AXIS_DIM)) < (ROPE_AXIS_DIM // 2)

    def norm_rope(y, w):
        yn = y * lax.rsqrt(jnp.mean(y * y, axis=-1, keepdims=True) + EPS) * w
        swapped = jnp.where(first_half,
                            pltpu.roll(yn, HEAD_DIM - ROPE_AXIS_DIM // 2, 1),
                            pltpu.roll(yn, ROPE_AXIS_DIM // 2, 1))
        return yn * cos + swapped * sin

    col = 0
    for c0 in range(0, QK_WIDTH, 2 * HEAD_DIM):
        y = _dot(h, w_ref[:, c0:c0 + 2 * HEAD_DIM])
        for j in range(2):
            hc = c0 + j * HEAD_DIM
            yh = y[:, j * HEAD_DIM:(j + 1) * HEAD_DIM]
            if hc < A_WIDTH:
                out = norm_rope(yh, qkw_ref[0:1, :]) * (HEAD_DIM ** -0.5)
                aq_ref[:, hc:hc + HEAD_DIM] = out.astype(BF16)
            else:
                out = norm_rope(yh, qkw_ref[1:2, :])
                ak_ref[:, hc - A_WIDTH:hc - A_WIDTH + HEAD_DIM] = out.astype(BF16)
    col = QK_WIDTH
    av_ref[...] = _dot(h, w_ref[:, col:col + A_KV_WIDTH]).astype(BF16)
    col += A_KV_WIDTH
    for c0 in range(0, B_QKV_WIDTH, 512):
        bqkv_ref[:, c0:c0 + 512] = _dot(h, w_ref[:, col + c0:col + c0 + 512])
    col += B_QKV_WIDTH
    gb_ref[...] = _dot(h, w_ref[:, col:col + GATE_WIDTH])
    col += GATE_WIDTH
    mq_ref[...] = (_dot(h, w_ref[:, col:col + M_WIDTH]) * (HEAD_DIM ** -0.5)).astype(BF16)
    col += M_WIDTH
    for c0 in range(0, D_MIX, 512):
        z_ref[:, c0:c0 + 512] = _dot(h, w_ref[:, col + c0:col + c0 + 512]).astype(BF16)


def _in_proj(x2, nw, w_all, qkw, cos_t, sin_t, seq, tm):
    rows = x2.shape[0]
    n_seq_tiles = seq // tm
    wtot = w_all.shape[1]
    row = lambda i: (i, 0)
    const = lambda i: (0, 0)
    outs = [
        (A_WIDTH, BF16), (A_KV_WIDTH, BF16), (A_KV_WIDTH, BF16), (B_QKV_WIDTH, F32),
        (GATE_WIDTH, F32), (M_WIDTH, BF16), (D_MIX, BF16),
    ]
    return pl.pallas_call(
        _in_proj_body,
        grid=(rows // tm,),
        in_specs=[
            pl.BlockSpec((tm, D_MODEL), row),
            pl.BlockSpec((1, D_MODEL), const),
            pl.BlockSpec((D_MODEL, wtot), const, pipeline_mode=pl.Buffered(1)),
            pl.BlockSpec((SUBLANES, HEAD_DIM), const),
            pl.BlockSpec((tm, HEAD_DIM), lambda i: (i % n_seq_tiles, 0)),
            pl.BlockSpec((tm, HEAD_DIM), lambda i: (i % n_seq_tiles, 0)),
        ],
        out_specs=[pl.BlockSpec((tm, w), row) for w, _ in outs],
        out_shape=[jax.ShapeDtypeStruct((rows, w), dt) for w, dt in outs],
        compiler_params=_params(("parallel",)),
        name="in_proj",
    )(x2, nw, w_all, qkw, cos_t, sin_t)


def _attn_body(q_ref, k_ref, v_ref, z_ref, o_ref, *, tq, tk, seq):
    q = jnp.concatenate([q_ref[:, g * HEAD_DIM:(g + 1) * HEAD_DIM] for g in range(A_GROUP)], axis=0)
    rows = A_GROUP * tq

    def step(j, carry):
        m, l, acc = carry
        start = pl.multiple_of(j * tk, tk)
        kt = k_ref[pl.ds(start, tk), :]
        vt = v_ref[pl.ds(start, tk), :]
        s = _dot(q, kt, _NT)
        m_new = jnp.maximum(m, jnp.max(s, axis=-1, keepdims=True))
        alpha = jnp.exp(m - m_new)
        p = jnp.exp(s - m_new)
        l = alpha * l + jnp.sum(p, axis=-1, keepdims=True)
        acc = alpha * acc + _dot(p.astype(BF16), vt)
        return m_new, l, acc

    init = (jnp.full((rows, 1), -jnp.inf, F32), jnp.zeros((rows, 1), F32), jnp.zeros((rows, HEAD_DIM), F32))
    _, l, acc = lax.fori_loop(0, seq // tk, step, init)
    y = acc / l
    for g in range(A_GROUP):
        zg = z_ref[:, g * HEAD_DIM:(g + 1) * HEAD_DIM].astype(F32)
        o_ref[:, g * HEAD_DIM:(g + 1) * HEAD_DIM] = (y[g * tq:(g + 1) * tq] * _silu(zg)).astype(BF16)


def _attention(aq, ak, av, z, tq, tk):
    b, seq, _ = aq.shape
    gw = A_GROUP * HEAD_DIM
    return pl.pallas_call(
        functools.partial(_attn_body, tq=tq, tk=tk, seq=seq),
        grid=(b, A_KV_HEADS, seq // tq),
        in_specs=[
            pl.BlockSpec((None, tq, gw), lambda bi, kv, i: (bi, i, kv)),
            pl.BlockSpec((None, seq, HEAD_DIM), lambda bi, kv, i: (bi, 0, kv)),
            pl.BlockSpec((None, seq, HEAD_DIM), lambda bi, kv, i: (bi, 0, kv)),
            pl.BlockSpec((None, tq, gw), lambda bi, kv, i: (bi, i, kv)),
        ],
        out_specs=pl.BlockSpec((None, tq, gw), lambda bi, kv, i: (bi, i, kv)),
        out_shape=jax.ShapeDtypeStruct((b, seq, A_WIDTH), BF16),
        compiler_params=_params(("parallel", "parallel", "arbitrary")),
        name="attn",
    )(aq, ak, av, z)


def _gdn_prep_body(cur_ref, prev_ref, next_ref, cw_ref, gb_ref, alog_ref, dtb_ref,
                   qkv_ref, gate_ref, ext_ref, *, tm, n_tiles):
    i = pl.program_id(1)
    halo = SUBLANES
    prev = prev_ref[...]
    nxt = next_ref[...]
    ext_ref[0:halo, :] = jnp.where(i == 0, jnp.zeros_like(prev), prev)
    ext_ref[halo:halo + tm, :] = cur_ref[...]
    ext_ref[halo + tm:, :] = jnp.where(i == n_tiles - 1, jnp.zeros_like(nxt), nxt)

    y = None
    for j in range(CONV_K):
        term = ext_ref[pl.ds(halo - CONV_K // 2 + j, tm), :] * cw_ref[j:j + 1, :]
        y = term if y is None else y + term
    y = _silu(y)

    for hh in range(2 * B_HEADS):
        c0 = hh * HEAD_DIM
        yh = y[:, c0:c0 + HEAD_DIM]
        yn = yh * lax.rsqrt(jnp.sum(yh * yh, axis=-1, keepdims=True) + EPS)
        if hh < B_HEADS:
            yn = yn * (HEAD_DIM ** -0.5)
        qkv_ref[:, c0:c0 + HEAD_DIM] = yn.astype(BF16)
    qkv_ref[:, 2 * B_WIDTH:] = y[:, 2 * B_WIDTH:].astype(BF16)

    g = gb_ref[...]
    lane = lax.broadcasted_iota(jnp.int32, g.shape, 1) % HEAD_DIM
    t = g + dtb_ref[...]
    softplus = jnp.maximum(t, 0.0) + jnp.log1p(jnp.exp(-jnp.abs(t)))
    glog = -jnp.exp(alog_ref[...]) * softplus
    beta = jax.nn.sigmoid(g)
    gate_ref[...] = jnp.where(lane < B_HEADS, glog, jnp.where(lane < 2 * B_HEADS, beta, 0.0))


def _gdn_prep(bqkv, cw, gb, alog_v, dtb_v, tm):
    b, seq, _ = bqkv.shape
    n_tiles = seq // tm
    per = tm // SUBLANES
    last_blk = seq // SUBLANES - 1
    return pl.pallas_call(
        functools.partial(_gdn_prep_body, tm=tm, n_tiles=n_tiles),
        grid=(b, n_tiles),
        in_specs=[
            pl.BlockSpec((None, tm, B_QKV_WIDTH), lambda bi, i: (bi, i, 0)),
            pl.BlockSpec((None, SUBLANES, B_QKV_WIDTH), lambda bi, i: (bi, jnp.maximum(i * per - 1, 0), 0)),
            pl.BlockSpec((None, SUBLANES, B_QKV_WIDTH), lambda bi, i: (bi, jnp.minimum((i + 1) * per, last_blk), 0)),
            pl.BlockSpec((SUBLANES, B_QKV_WIDTH), lambda bi, i: (0, 0)),
            pl.BlockSpec((None, tm, GATE_WIDTH), lambda bi, i: (bi, i, 0)),
            pl.BlockSpec((1, GATE_WIDTH), lambda bi, i: (0, 0)),
            pl.BlockSpec((1, GATE_WIDTH), lambda bi, i: (0, 0)),
        ],
        out_specs=[
            pl.BlockSpec((None, tm, B_QKV_WIDTH), lambda bi, i: (bi, i, 0)),
            pl.BlockSpec((None, tm, GATE_WIDTH), lambda bi, i: (bi, i, 0)),
        ],
        out_shape=[
            jax.ShapeDtypeStruct((b, seq, B_QKV_WIDTH), BF16),
            jax.ShapeDtypeStruct((b, seq, GATE_WIDTH), F32),
        ],
        scratch_shapes=[pltpu.VMEM((tm + 2 * SUBLANES, B_QKV_WIDTH), F32)],
        compiler_params=_params(("parallel", "parallel")),
        name="gdn_prep",
    )(bqkv, bqkv, bqkv, cw, gb, alog_v, dtb_v)


def _split3(a):
    hi = a.astype(BF16)
    r1 = a - hi.astype(F32)
    mid = r1.astype(BF16)
    lo = (r1 - mid.astype(F32)).astype(BF16)
    return [hi, mid, lo]


def _lane_blocks(cols, width, block):
    rows = cols[0].shape[0]
    blk = lax.broadcasted_iota(jnp.int32, (rows, width), 1) // block
    out = jnp.broadcast_to(cols[-1], (rows, width))
    for i in range(len(cols) - 2, -1, -1):
        out = jnp.where(blk == i, cols[i], out)
    return out


def _tile_rows(x, n):
    return jnp.concatenate([x] * n, axis=0)


def _gdn_chunk_body(q_ref, k_ref, v_ref, g_ref, mqf_ref, bof_ref, dcf_ref, mqb_ref, bob_ref, dcb_ref, *, gsz):
    c_, h_, w_ = CHUNK, B_HEADS, B_HEADS * CHUNK
    ri = lax.broadcasted_iota(jnp.int32, (c_, w_), 0)
    jj = lax.broadcasted_iota(jnp.int32, (c_, w_), 1) % c_
    eye_w = (ri == jj).astype(F32)
    r2 = lax.broadcasted_iota(jnp.int32, (c_, c_), 0)
    c2 = lax.broadcasted_iota(jnp.int32, (c_, c_), 1)

    def blockdiag_mask(width, lanes_per_head, period=None):
        rh = lax.broadcasted_iota(jnp.int32, (w_, width), 0) // c_
        ln = lax.broadcasted_iota(jnp.int32, (w_, width), 1)
        if period is not None:
            ln = ln % period
        return (rh == ln // lanes_per_head).astype(BF16)

    bd_mask = blockdiag_mask(w_, c_)
    bdk_mask = blockdiag_mask(B_WIDTH, HEAD_DIM)
    bdr_mask = blockdiag_mask(2 * B_WIDTH, HEAD_DIM, period=B_WIDTH)

    dirs = []
    for d in range(N_DIR):
        diff = (ri - jj) if d == 0 else (jj - ri)
        d2 = (r2 - c2) if d == 0 else (c2 - r2)
        dirs.append(dict(incl=diff >= 0, strict=diff > 0, m2=(diff > 0).astype(F32),
                         m1=(d2 >= 0).astype(BF16)))
    outs = ((mqf_ref, bof_ref, dcf_ref), (mqb_ref, bob_ref, dcb_ref))
    row8 = lax.broadcasted_iota(jnp.int32, (SUBLANES, HEAD_DIM), 0)

    for g in range(gsz):
        sl = slice(g * c_, (g + 1) * c_)
        kc = k_ref[sl, :]
        qc = q_ref[sl, :]
        vc = v_ref[sl, :]
        bdk = _tile_rows(kc, h_) * bdk_mask
        kq = _dot(jnp.concatenate([kc, qc], axis=0), bdk, _NT)
        kk_w = kq[:c_]
        qk_w = kq[c_:]
        kf = kc.astype(F32)
        qf = qc.astype(F32)
        vf = vc.astype(F32)
        for d in range(N_DIR):
            m = dirs[d]
            mq_ref, bo_ref, dc_ref = outs[d]
            slot = g if d == 0 else gsz - 1 - g
            gt = g_ref[sl, d * HEAD_DIM:(d + 1) * HEAD_DIM]
            gl_w = _lane_blocks([gt[:, h:h + 1] for h in range(h_)], w_, c_)
            d3 = _dot(m["m1"], jnp.concatenate(_split3(gl_w * m["m2"]) + _split3(gt), axis=1))
            dm_w = d3[:, :w_] + d3[:, w_:2 * w_] + d3[:, 2 * w_:3 * w_]
            o3 = 3 * w_
            gcum = d3[:, o3:o3 + HEAD_DIM] + d3[:, o3 + HEAD_DIM:o3 + 2 * HEAD_DIM] + d3[:, o3 + 2 * HEAD_DIM:]
            decay = jnp.exp(dm_w)
            beta_w = _lane_blocks([gt[:, h_ + h:h_ + h + 1] for h in range(h_)], w_, c_)
            x = -jnp.where(m["strict"], kk_w * beta_w * decay, 0.0)
            xb = x.astype(BF16)
            xp = _dot(xb, _tile_rows(xb, h_) * bd_mask)
            p = eye_w + x
            for _ in range(4):
                xpb = xp.astype(BF16)
                r = _dot(jnp.concatenate([p.astype(BF16), xpb], axis=0), _tile_rows(xpb, h_) * bd_mask)
                p = p + r[:c_]
                xp = r[c_:]
            p = p + _dot(p.astype(BF16), _tile_rows(xp.astype(BF16), h_) * bd_mask)

            g_hd = _lane_blocks([gcum[:, h:h + 1] for h in range(h_)], B_WIDTH, HEAD_DIM)
            beta_hd = _lane_blocks([gt[:, h_ + h:h_ + h + 1] for h in range(h_)], B_WIDTH, HEAD_DIM)
            gtot = jnp.sum(gt, axis=0, keepdims=True)
            gtot_hd = _lane_blocks([gtot[:, h:h + 1] for h in range(h_)], B_WIDTH, HEAD_DIM)
            eg_hd = jnp.exp(g_hd)
            rhs = jnp.concatenate([vf * beta_hd, kf * (beta_hd * eg_hd)], axis=1).astype(BF16)
            uw = _dot(p.astype(BF16), _tile_rows(rhs, h_) * bdr_mask)
            intra = jnp.where(m["incl"], qk_w * decay, 0.0).astype(BF16)
            iuw = _dot(intra, _tile_rows(uw.astype(BF16), h_) * bdr_mask)
            o_local = iuw[:, :B_WIDTH]
            q_eff = qf * eg_hd - iuw[:, B_WIDTH:]
            k_tail = (kf * jnp.exp(gtot_hd - g_hd)).astype(BF16)
            dc_tile = jnp.zeros((SUBLANES, HEAD_DIM), F32)
            for h in range(h_):
                hs = slice(h * HEAD_DIM, (h + 1) * HEAD_DIM)
                wu = jnp.concatenate([-uw[:, B_WIDTH + h * HEAD_DIM:B_WIDTH + (h + 1) * HEAD_DIM], uw[:, hs]],
                                     axis=1).astype(BF16)
                mb = _dot(k_tail[:, hs], wu, _TN)
                mq_ref[slot, h, 0:HEAD_DIM, :] = mb[:, :HEAD_DIM].astype(BF16)
                mq_ref[slot, h, HEAD_DIM:, :] = q_eff[:, hs].astype(BF16)
                bo_ref[slot, h, 0:HEAD_DIM, :] = mb[:, HEAD_DIM:].astype(BF16)
                bo_ref[slot, h, HEAD_DIM:, :] = o_local[:, hs].astype(BF16)
                dc_tile = jnp.where(row8 == h, jnp.exp(gtot_hd[:, hs]), dc_tile)
            dc_ref[slot] = dc_tile


def _gdn_chunk(qkvn, gates, gsz):
    b, seq, _ = qkvn.shape
    nc = seq // CHUNK
    nb = nc // gsz
    rows = gsz * CHUNK
    mrows = HEAD_DIM + CHUNK
    fwd5 = lambda bi, i: (bi, i, 0, 0, 0)
    bwd5 = lambda bi, i: (bi, nb - 1 - i, 0, 0, 0)
    fwd4 = lambda bi, i: (bi, i, 0, 0)
    bwd4 = lambda bi, i: (bi, nb - 1 - i, 0, 0)
    big = jax.ShapeDtypeStruct((b, nc, B_HEADS, mrows, HEAD_DIM), BF16)
    small = jax.ShapeDtypeStruct((b, nc, SUBLANES, HEAD_DIM), F32)
    big_blk = (None, gsz, B_HEADS, mrows, HEAD_DIM)
    small_blk = (None, gsz, SUBLANES, HEAD_DIM)
    return pl.pallas_call(
        functools.partial(_gdn_chunk_body, gsz=gsz),
        grid=(b, nb),
        in_specs=[
            pl.BlockSpec((None, rows, B_WIDTH), lambda bi, i: (bi, i, 0)),
            pl.BlockSpec((None, rows, B_WIDTH), lambda bi, i: (bi, i, 1)),
            pl.BlockSpec((None, rows, B_WIDTH), lambda bi, i: (bi, i, 2)),
            pl.BlockSpec((None, rows, GATE_WIDTH), lambda bi, i: (bi, i, 0)),
        ],
        out_specs=[
            pl.BlockSpec(big_blk, fwd5), pl.BlockSpec(big_blk, fwd5), pl.BlockSpec(small_blk, fwd4),
            pl.BlockSpec(big_blk, bwd5), pl.BlockSpec(big_blk, bwd5), pl.BlockSpec(small_blk, bwd4),
        ],
        out_shape=[big, big, small, big, big, small],
        compiler_params=_params(("parallel", "parallel")),
        name="gdn_chunk",
    )(qkvn, qkvn, qkvn, gates)


def _gdn_scan_body(mqf_ref, bof_ref, dcf_ref, mqb_ref, bob_ref, dcb_ref, of_ref, ob_ref, s_ref, *, gsz):
    @pl.when(pl.program_id(1) == 0)
    def _():
        s_ref[...] = jnp.zeros_like(s_ref)

    ins = ((mqf_ref, bof_ref, dcf_ref, of_ref), (mqb_ref, bob_ref, dcb_ref, ob_ref))
    for g in range(gsz):
        for d in range(N_DIR):
            mq_ref, bo_ref, dc_ref, o_ref = ins[d]
            out_slot = g if d == 0 else gsz - 1 - g
            for h in range(B_HEADS):
                state = s_ref[d * B_HEADS + h]
                r = _dot(mq_ref[g, h], state.astype(BF16))
                bo = bo_ref[g, h].astype(F32)
                s_ref[d * B_HEADS + h] = state * dc_ref[g, h:h + 1, :] + r[:HEAD_DIM] + bo[:HEAD_DIM]
                o_ref[out_slot * CHUNK:(out_slot + 1) * CHUNK, h * HEAD_DIM:(h + 1) * HEAD_DIM] = (
                    r[HEAD_DIM:] + bo[HEAD_DIM:])


def _gdn_scan(ops, seq, gsz):
    mqf = ops[0]
    b, nc = mqf.shape[:2]
    nb = nc // gsz
    rows = gsz * CHUNK
    mrows = HEAD_DIM + CHUNK
    big_blk = (None, gsz, B_HEADS, mrows, HEAD_DIM)
    small_blk = (None, gsz, SUBLANES, HEAD_DIM)
    five = lambda bi, i: (bi, i, 0, 0, 0)
    four = lambda bi, i: (bi, i, 0, 0)
    out = jax.ShapeDtypeStruct((b, seq, B_WIDTH), F32)
    return pl.pallas_call(
        functools.partial(_gdn_scan_body, gsz=gsz),
        grid=(b, nb),
        in_specs=[pl.BlockSpec(big_blk, five), pl.BlockSpec(big_blk, five), pl.BlockSpec(small_blk, four)] * 2,
        out_specs=[
            pl.BlockSpec((None, rows, B_WIDTH), lambda bi, i: (bi, i, 0)),
            pl.BlockSpec((None, rows, B_WIDTH), lambda bi, i: (bi, nb - 1 - i, 0)),
        ],
        out_shape=[out, out],
        scratch_shapes=[pltpu.VMEM((N_DIR * B_HEADS, HEAD_DIM, HEAD_DIM), F32)],
        compiler_params=_params(("parallel", "arbitrary")),
        name="gdn_scan",
    )(*ops)


def _mem_kv_body(m_ref, nw_ref, w_ref, o_ref):
    x = m_ref[...]
    ms = jnp.mean(x * x, axis=-1, keepdims=True)
    h = (x * lax.rsqrt(ms + EPS) * nw_ref[...]).astype(BF16)
    o_ref[...] = _dot(h, w_ref[...]).astype(BF16)


def _mem_kv(mem2, nw, w, tm):
    rows = mem2.shape[0]
    return pl.pallas_call(
        _mem_kv_body,
        grid=(rows // tm,),
        in_specs=[
            pl.BlockSpec((tm, D_MODEL), lambda i: (i, 0)),
            pl.BlockSpec((1, D_MODEL), lambda i: (0, 0)),
            pl.BlockSpec((D_MODEL, 2 * M_WIDTH), lambda i: (0, 0)),
        ],
        out_specs=pl.BlockSpec((tm, 2 * M_WIDTH), lambda i: (i, 0)),
        out_shape=jax.ShapeDtypeStruct((rows, 2 * M_WIDTH), BF16),
        compiler_params=_params(("parallel",)),
        name="mem_kv",
    )(mem2, nw, w)


def _mem_attn_body(q_ref, kv_ref, z_ref, o_ref):
    for h in range(M_HEADS):
        hs = slice(h * HEAD_DIM, (h + 1) * HEAD_DIM)
        s = _dot(q_ref[:, hs], kv_ref[:, hs], _NT)
        m = jnp.max(s, axis=-1, keepdims=True)
        p = jnp.exp(s - m)
        l = jnp.sum(p, axis=-1, keepdims=True)
        vh = kv_ref[:, M_WIDTH + h * HEAD_DIM:M_WIDTH + (h + 1) * HEAD_DIM]
        y = _dot(p.astype(BF16), vh) / l
        o_ref[:, hs] = (y * _silu(z_ref[:, hs].astype(F32))).astype(BF16)


def _mem_attn(mq, mkv, z, tq):
    b, seq, _ = mq.shape
    n_mem = mkv.shape[1]
    z_blk = (A_WIDTH + B_WIDTH) // M_WIDTH
    return pl.pallas_call(
        _mem_attn_body,
        grid=(b, seq // tq),
        in_specs=[
            pl.BlockSpec((None, tq, M_WIDTH), lambda bi, i: (bi, i, 0)),
            pl.BlockSpec((None, n_mem, 2 * M_WIDTH), lambda bi, i: (bi, 0, 0)),
            pl.BlockSpec((None, tq, M_WIDTH), lambda bi, i: (bi, i, z_blk)),
        ],
        out_specs=pl.BlockSpec((None, tq, M_WIDTH), lambda bi, i: (bi, i, 0)),
        out_shape=jax.ShapeDtypeStruct((b, seq, M_WIDTH), BF16),
        compiler_params=_params(("parallel", "parallel")),
        name="mem_attn",
    )(mq, mkv, z)


def _out_proj_body(ya_ref, of_ref, ob_ref, zb_ref, ym_ref, dnw_ref, w_ref, pw_ref, x_ref, o_ref):
    ob = of_ref[...] + ob_ref[...]
    dnw = dnw_ref[...]
    parts = []
    for h in range(B_HEADS):
        hs = slice(h * HEAD_DIM, (h + 1) * HEAD_DIM)
        oh = ob[:, hs]
        yn = oh * lax.rsqrt(jnp.mean(oh * oh, axis=-1, keepdims=True) + EPS) * dnw
        parts.append((yn * _silu(zb_ref[:, hs].astype(F32))).astype(BF16))
    yb = jnp.concatenate(parts, axis=1)
    acc = _dot(ya_ref[...], w_ref[0:A_WIDTH, :])
    acc = acc + _dot(yb, w_ref[A_WIDTH:A_WIDTH + B_WIDTH, :])
    acc = acc + _dot(ym_ref[...], w_ref[A_WIDTH + B_WIDTH:, :])
    y = acc * lax.rsqrt(jnp.mean(acc * acc, axis=-1, keepdims=True) + EPS) * pw_ref[...]
    o_ref[...] = x_ref[...] + y


def _out_proj(ya, o_f, o_b, z, ym, dnw, w, pw, x, tm):
    b, seq, _ = ya.shape
    z_blk = A_WIDTH // B_WIDTH
    row = lambda bi, i: (bi, i, 0)
    const = lambda bi, i: (0, 0)
    return pl.pallas_call(
        _out_proj_body,
        grid=(b, seq // tm),
        in_specs=[
            pl.BlockSpec((None, tm, A_WIDTH), row),
            pl.BlockSpec((None, tm, B_WIDTH), row),
            pl.BlockSpec((None, tm, B_WIDTH), row),
            pl.BlockSpec((None, tm, B_WIDTH), lambda bi, i: (bi, i, z_blk)),
            pl.BlockSpec((None, tm, M_WIDTH), row),
            pl.BlockSpec((1, HEAD_DIM), const),
            pl.BlockSpec((D_MIX, D_MODEL), const),
            pl.BlockSpec((1, D_MODEL), const),
            pl.BlockSpec((None, tm, D_MODEL), row),
        ],
        out_specs=pl.BlockSpec((None, tm, D_MODEL), row),
        out_shape=jax.ShapeDtypeStruct((b, seq, D_MODEL), F32),
        compiler_params=_params(("parallel", "parallel")),
        name="out_proj",
    )(ya, o_f, o_b, z, ym, dnw, w, pw, x)


def _rope_tables(seq):
    t = np.arange(seq)
    inv_freq = ROPE_THETA ** (-jnp.arange(0, ROPE_AXIS_DIM, 2, dtype=F32) / ROPE_AXIS_DIM)
    ang_row = jnp.asarray(t // GRID_W, F32)[:, None] * inv_freq
    ang_col = jnp.asarray(t % GRID_W, F32)[:, None] * inv_freq
    cr, sr, cc, sc = jnp.cos(ang_row), jnp.sin(ang_row), jnp.cos(ang_col), jnp.sin(ang_col)
    return (jnp.concatenate([cr, cr, cc, cc], axis=-1),
            jnp.concatenate([-sr, sr, -sc, sc], axis=-1))


def _arrange_w_in(w):
    pts = np.cumsum([A_WIDTH, A_KV_WIDTH, A_KV_WIDTH, B_QKV_WIDTH, N_DIR * B_HEADS, N_DIR * B_HEADS, M_WIDTH])
    aq, ak, av, bqkv, ba, bb, mq, z = jnp.split(w, pts, axis=1)
    gate_blocks = []
    for d in range(N_DIR):
        sl = slice(d * B_HEADS, (d + 1) * B_HEADS)
        pad = jnp.zeros((w.shape[0], HEAD_DIM - 2 * B_HEADS), w.dtype)
        gate_blocks += [ba[:, sl], bb[:, sl], pad]
    return jnp.concatenate([aq, ak, av, bqkv] + gate_blocks + [mq, z], axis=1).astype(BF16)


def _gate_lane_vec(p):
    out = jnp.zeros((N_DIR, HEAD_DIM), F32).at[:, :B_HEADS].set(p.astype(F32))
    return out.reshape(1, GATE_WIDTH)


def kernel(x, mem, norm_pre_w, w_in, q_norm_w, k_norm_w, conv_w, a_log, dt_bias, delta_norm_w,
           mem_norm_w, w_mem_kv, w_out, norm_post_w):
    b, seq, _ = x.shape
    n_mem = mem.shape[1]
    assert seq % 512 == 0 and seq % GRID_W == 0 and w_in.shape[0] == 1
    l = 0
    tm = 512

    cos_t, sin_t = _rope_tables(seq)
    qkw = jnp.zeros((SUBLANES, HEAD_DIM), F32).at[0].set(q_norm_w[l]).at[1].set(k_norm_w[l])
    aq, ak, av, bqkv, gb, mq, z = _in_proj(
        x.reshape(b * seq, D_MODEL), norm_pre_w[l].reshape(1, D_MODEL), _arrange_w_in(w_in[l]),
        qkw, cos_t, sin_t, seq, tm)
    shp = lambda a: a.reshape(b, seq, a.shape[-1])
    aq, ak, av, bqkv, gb, mq, z = map(shp, (aq, ak, av, bqkv, gb, mq, z))

    ya = _attention(aq, ak, av, z, tq=128, tk=512)

    cw = jnp.zeros((SUBLANES, B_QKV_WIDTH), F32).at[:CONV_K].set(conv_w[l])
    qkvn, gates = _gdn_prep(bqkv, cw, gb, _gate_lane_vec(a_log[l]), _gate_lane_vec(dt_bias[l]), tm=256)
    o_f, o_b = _gdn_scan(_gdn_chunk(qkvn, gates, gsz=4), seq, gsz=4)

    mkv = _mem_kv(mem.reshape(b * n_mem, D_MODEL), mem_norm_w[l].reshape(1, D_MODEL),
                  w_mem_kv[l].astype(BF16), tm=256)
    ym = _mem_attn(mq, mkv.reshape(b, n_mem, 2 * M_WIDTH), z, tq=512)

    return _out_proj(ya, o_f, o_b, z, ym, delta_norm_w[l].reshape(1, HEAD_DIM), w_out[l].astype(BF16),
                     norm_post_w[l].reshape(1, D_MODEL), x, tm=256)
```

```python
import functools

import jax
import jax.numpy as jnp
import numpy as np
from jax import lax
from jax.experimental import pallas as pl
from jax.experimental.pallas import tpu as pltpu

F32 = jnp.float32
BF16 = jnp.bfloat16

D_MODEL = 1024
HEAD_DIM = 128
GRID_W = 64
A_HEADS = 8
A_KV_HEADS = 2
A_GROUP = A_HEADS // A_KV_HEADS
A_WIDTH = A_HEADS * HEAD_DIM
A_KV_WIDTH = A_KV_HEADS * HEAD_DIM
B_HEADS = 4
B_WIDTH = B_HEADS * HEAD_DIM
B_QKV_WIDTH = 3 * B_WIDTH
N_DIR = 2
CONV_K = 5
CHUNK = 64
M_HEADS = 4
M_WIDTH = M_HEADS * HEAD_DIM
D_MIX = A_WIDTH + B_WIDTH + M_WIDTH
ROPE_AXIS_DIM = HEAD_DIM // 2
ROPE_THETA = 10000.0
EPS = 1e-6
LOG2E = 1.4426950408889634
GATE_WIDTH = N_DIR * HEAD_DIM
QK_WIDTH = A_WIDTH + A_KV_WIDTH
SUBLANES = 8
VMEM_LIMIT = 56 * 1024 * 1024

_NT = (((1,), (1,)), ((), ()))
_TN = (((0,), (0,)), ((), ()))


def _dot(a, b, dims=None, precision=None):
    if dims is None:
        return jnp.dot(a, b, preferred_element_type=F32, precision=precision)
    return lax.dot_general(a, b, dims, preferred_element_type=F32, precision=precision)


def _silu(x):
    return x * jax.nn.sigmoid(x)


def _params(sem):
    return pltpu.CompilerParams(dimension_semantics=sem, vmem_limit_bytes=VMEM_LIMIT)


def _in_proj_body(x_ref, nw_ref, w_ref, qkw_ref, cos_ref, sin_ref,
                  aq_ref, ak_ref, av_ref, bqkv_ref, gb_ref, mq_ref, z_ref):
    x = x_ref[...]
    ms = jnp.mean(x * x, axis=-1, keepdims=True)
    h = (x * lax.rsqrt(ms + EPS) * nw_ref[...]).astype(BF16)

    cos = cos_ref[...]
    sin = sin_ref[...]
    lane = lax.broadcasted_iota(jnp.int32, cos.shape, 1)
    first_half = (lane % (ROPE_AXIS_DIM)) < (ROPE_AXIS_DIM // 2)

    def norm_rope(y, w):
        yn = y * lax.rsqrt(jnp.mean(y * y, axis=-1, keepdims=True) + EPS) * w
        swapped = jnp.where(first_half,
                            pltpu.roll(yn, HEAD_DIM - ROPE_AXIS_DIM // 2, 1),
                            pltpu.roll(yn, ROPE_AXIS_DIM // 2, 1))
        return yn * cos + swapped * sin

    col = 0
    for c0 in range(0, QK_WIDTH, 2 * HEAD_DIM):
        y = _dot(h, w_ref[:, c0:c0 + 2 * HEAD_DIM])
        for j in range(2):
            hc = c0 + j * HEAD_DIM
            yh = y[:, j * HEAD_DIM:(j + 1) * HEAD_DIM]
            if hc < A_WIDTH:
                out = norm_rope(yh, qkw_ref[0:1, :]) * (HEAD_DIM ** -0.5 * LOG2E)
                aq_ref[hc:hc + HEAD_DIM, :] = out.T.astype(BF16)
            else:
                out = norm_rope(yh, qkw_ref[1:2, :])
                ak_ref[:, hc - A_WIDTH:hc - A_WIDTH + HEAD_DIM] = out.astype(BF16)
    col = QK_WIDTH
    av_ref[...] = _dot(h, w_ref[:, col:col + A_KV_WIDTH]).T.astype(BF16)
    col += A_KV_WIDTH
    for c0 in range(0, B_QKV_WIDTH, 512):
        bqkv_ref[:, c0:c0 + 512] = _dot(h, w_ref[:, col + c0:col + c0 + 512])
    col += B_QKV_WIDTH
    gb_ref[...] = _dot(h, w_ref[:, col:col + GATE_WIDTH])
    col += GATE_WIDTH
    mq_ref[...] = (_dot(h, w_ref[:, col:col + M_WIDTH]) * (HEAD_DIM ** -0.5)).astype(BF16)
    col += M_WIDTH
    for c0 in range(0, D_MIX, 512):
        z_ref[:, c0:c0 + 512] = _dot(h, w_ref[:, col + c0:col + c0 + 512]).astype(BF16)


def _in_proj(x2, nw, w_all, qkw, cos_t, sin_t, seq, tm):
    rows = x2.shape[0]
    n_seq_tiles = seq // tm
    wtot = w_all.shape[1]
    row = lambda i: (i, 0)
    const = lambda i: (0, 0)
    nb = rows // seq
    outs = [
        (A_KV_WIDTH, BF16), (B_QKV_WIDTH, F32), (GATE_WIDTH, F32), (M_WIDTH, BF16), (D_MIX, BF16),
    ]
    t_specs = [
        pl.BlockSpec((None, A_WIDTH, tm), lambda i: (i // n_seq_tiles, 0, i % n_seq_tiles)),
        pl.BlockSpec((None, None, A_KV_WIDTH, tm), lambda i: (i // n_seq_tiles, i % n_seq_tiles, 0, 0)),
    ]
    t_shapes = [
        jax.ShapeDtypeStruct((nb, A_WIDTH, seq), BF16),
        jax.ShapeDtypeStruct((nb, n_seq_tiles, A_KV_WIDTH, tm), BF16),
    ]
    return pl.pallas_call(
        _in_proj_body,
        grid=(rows // tm,),
        in_specs=[
            pl.BlockSpec((tm, D_MODEL), row),
            pl.BlockSpec((1, D_MODEL), const),
            pl.BlockSpec((D_MODEL, wtot), const, pipeline_mode=pl.Buffered(1)),
            pl.BlockSpec((SUBLANES, HEAD_DIM), const),
            pl.BlockSpec((tm, HEAD_DIM), lambda i: (i % n_seq_tiles, 0)),
            pl.BlockSpec((tm, HEAD_DIM), lambda i: (i % n_seq_tiles, 0)),
        ],
        out_specs=[t_specs[0], pl.BlockSpec((tm, outs[0][0]), row), t_specs[1]]
        + [pl.BlockSpec((tm, w), row) for w, _ in outs[1:]],
        out_shape=[t_shapes[0], jax.ShapeDtypeStruct((rows, outs[0][0]), outs[0][1]), t_shapes[1]]
        + [jax.ShapeDtypeStruct((rows, w), dt) for w, dt in outs[1:]],
        compiler_params=_params(("parallel",)),
        name="in_proj",
    )(x2, nw, w_all, qkw, cos_t, sin_t)


def _attn_body(qt_ref, k_ref, vt_ref, z_ref, o_ref, m_ref, l_ref, acc_ref, s_ref, *, n_kv):
    m_ref[...] = jnp.full(m_ref.shape, -jnp.inf, F32)
    l_ref[...] = jnp.zeros(l_ref.shape, F32)
    acc_ref[...] = jnp.zeros(acc_ref.shape, F32)

    def scores(j, slot):
        kt = k_ref[j]
        for g in range(A_GROUP):
            s_ref[slot, g] = _dot(kt, qt_ref[g * HEAD_DIM:(g + 1) * HEAD_DIM, :])

    def softmax_pv(j, slot):
        vt = vt_ref[j]
        for g in range(A_GROUP):
            s = s_ref[slot, g]
            m_old = m_ref[g]
            m_new = jnp.maximum(m_old, jnp.max(s, axis=0, keepdims=True))
            alpha = jnp.exp2(m_old - m_new)
            p = jnp.exp2(s - m_new)
            l_ref[g] = alpha * l_ref[g] + jnp.sum(p, axis=0, keepdims=True)
            acc_ref[g] = alpha * acc_ref[g] + _dot(vt, p.astype(BF16))
            m_ref[g] = m_new

    scores(0, 0)

    def pair(jj, carry):
        j = 2 * jj
        scores(j + 1, 1)
        softmax_pv(j, 0)
        scores(j + 2, 0)
        softmax_pv(j + 1, 1)
        return carry

    lax.fori_loop(0, n_kv // 2 - 1, pair, 0)
    scores(n_kv - 1, 1)
    softmax_pv(n_kv - 2, 0)
    softmax_pv(n_kv - 1, 1)
    for g in range(A_GROUP):
        hs = slice(g * HEAD_DIM, (g + 1) * HEAD_DIM)
        y = (acc_ref[g] / l_ref[g]).T
        o_ref[:, hs] = (y * _silu(z_ref[:, hs].astype(F32))).astype(BF16)


def _attention(aqt, ak, avt, z, tq):
    b, _, seq = aqt.shape
    n_kv, tk = avt.shape[1], avt.shape[3]
    gw = A_GROUP * HEAD_DIM
    akr = ak.reshape(b, n_kv, tk, A_KV_WIDTH)
    return pl.pallas_call(
        functools.partial(_attn_body, n_kv=n_kv),
        grid=(b, A_KV_HEADS, seq // tq),
        in_specs=[
            pl.BlockSpec((None, gw, tq), lambda bi, kv, i: (bi, kv, i)),
            pl.BlockSpec((None, n_kv, tk, HEAD_DIM), lambda bi, kv, i: (bi, 0, 0, kv)),
            pl.BlockSpec((None, n_kv, HEAD_DIM, tk), lambda bi, kv, i: (bi, 0, kv, 0)),
            pl.BlockSpec((None, tq, gw), lambda bi, kv, i: (bi, i, kv)),
        ],
        out_specs=pl.BlockSpec((None, tq, gw), lambda bi, kv, i: (bi, i, kv)),
        out_shape=jax.ShapeDtypeStruct((b, seq, A_WIDTH), BF16),
        scratch_shapes=[pltpu.VMEM((A_GROUP, 1, tq), F32), pltpu.VMEM((A_GROUP, 1, tq), F32),
                        pltpu.VMEM((A_GROUP, HEAD_DIM, tq), F32), pltpu.VMEM((2, A_GROUP, tk, tq), F32)],
        compiler_params=_params(("parallel", "parallel", "arbitrary")),
        name="attn",
    )(aqt, akr, avt, z)


def _gdn_prep_body(cur_ref, prev_ref, next_ref, cw_ref, gb_ref, alog_ref, dtb_ref,
                   qkv_ref, gate_ref, ext_ref, *, tm, n_tiles):
    i = pl.program_id(1)
    halo = SUBLANES
    prev = prev_ref[...]
    nxt = next_ref[...]
    ext_ref[0:halo, :] = jnp.where(i == 0, jnp.zeros_like(prev), prev)
    ext_ref[halo:halo + tm, :] = cur_ref[...]
    ext_ref[halo + tm:, :] = jnp.where(i == n_tiles - 1, jnp.zeros_like(nxt), nxt)

    y = None
    for j in range(CONV_K):
        term = ext_ref[pl.ds(halo - CONV_K // 2 + j, tm), :] * cw_ref[j:j + 1, :]
        y = term if y is None else y + term
    y = _silu(y)

    for hh in range(2 * B_HEADS):
        c0 = hh * HEAD_DIM
        yh = y[:, c0:c0 + HEAD_DIM]
        yn = yh * lax.rsqrt(jnp.sum(yh * yh, axis=-1, keepdims=True) + EPS)
        if hh < B_HEADS:
            yn = yn * (HEAD_DIM ** -0.5)
        qkv_ref[:, c0:c0 + HEAD_DIM] = yn.astype(BF16)
    qkv_ref[:, 2 * B_WIDTH:] = y[:, 2 * B_WIDTH:].astype(BF16)

    g = gb_ref[...]
    lane = lax.broadcasted_iota(jnp.int32, g.shape, 1) % HEAD_DIM
    t = g + dtb_ref[...]
    softplus = jnp.maximum(t, 0.0) + jnp.log1p(jnp.exp(-jnp.abs(t)))
    glog = -jnp.exp(alog_ref[...]) * softplus
    beta = jax.nn.sigmoid(g)
    gate_ref[...] = jnp.where(lane < B_HEADS, glog, jnp.where(lane < 2 * B_HEADS, beta, 0.0))


def _gdn_prep(bqkv, cw, gb, alog_v, dtb_v, tm):
    b, seq, _ = bqkv.shape
    n_tiles = seq // tm
    per = tm // SUBLANES
    last_blk = seq // SUBLANES - 1
    return pl.pallas_call(
        functools.partial(_gdn_prep_body, tm=tm, n_tiles=n_tiles),
        grid=(b, n_tiles),
        in_specs=[
            pl.BlockSpec((None, tm, B_QKV_WIDTH), lambda bi, i: (bi, i, 0)),
            pl.BlockSpec((None, SUBLANES, B_QKV_WIDTH), lambda bi, i: (bi, jnp.maximum(i * per - 1, 0), 0)),
            pl.BlockSpec((None, SUBLANES, B_QKV_WIDTH), lambda bi, i: (bi, jnp.minimum((i + 1) * per, last_blk), 0)),
            pl.BlockSpec((SUBLANES, B_QKV_WIDTH), lambda bi, i: (0, 0)),
            pl.BlockSpec((None, tm, GATE_WIDTH), lambda bi, i: (bi, i, 0)),
            pl.BlockSpec((1, GATE_WIDTH), lambda bi, i: (0, 0)),
            pl.BlockSpec((1, GATE_WIDTH), lambda bi, i: (0, 0)),
        ],
        out_specs=[
            pl.BlockSpec((None, tm, B_QKV_WIDTH), lambda bi, i: (bi, i, 0)),
            pl.BlockSpec((None, tm, GATE_WIDTH), lambda bi, i: (bi, i, 0)),
        ],
        out_shape=[
            jax.ShapeDtypeStruct((b, seq, B_QKV_WIDTH), BF16),
            jax.ShapeDtypeStruct((b, seq, GATE_WIDTH), F32),
        ],
        scratch_shapes=[pltpu.VMEM((tm + 2 * SUBLANES, B_QKV_WIDTH), F32)],
        compiler_params=_params(("parallel", "parallel")),
        name="gdn_prep",
    )(bqkv, bqkv, bqkv, cw, gb, alog_v, dtb_v)


def _split3(a):
    hi = a.astype(BF16)
    r1 = a - hi.astype(F32)
    mid = r1.astype(BF16)
    lo = (r1 - mid.astype(F32)).astype(BF16)
    return [hi, mid, lo]


def _lane_blocks(cols, width, block):
    rows = cols[0].shape[0]
    blk = lax.broadcasted_iota(jnp.int32, (rows, width), 1) // block
    out = jnp.broadcast_to(cols[-1], (rows, width))
    for i in range(len(cols) - 2, -1, -1):
        out = jnp.where(blk == i, cols[i], out)
    return out


def _tile_rows(x, n):
    return jnp.concatenate([x] * n, axis=0)


def _gdn_chunk_body(q_ref, k_ref, v_ref, g_ref, mqf_ref, bof_ref, dcf_ref, mqb_ref, bob_ref, dcb_ref, *, gsz):
    c_, h_, w_ = CHUNK, B_HEADS, B_HEADS * CHUNK
    ri = lax.broadcasted_iota(jnp.int32, (c_, w_), 0)
    jj = lax.broadcasted_iota(jnp.int32, (c_, w_), 1) % c_
    eye_w = (ri == jj).astype(F32)
    r2 = lax.broadcasted_iota(jnp.int32, (c_, c_), 0)
    c2 = lax.broadcasted_iota(jnp.int32, (c_, c_), 1)

    def blockdiag_mask(width, lanes_per_head, period=None):
        rh = lax.broadcasted_iota(jnp.int32, (w_, width), 0) // c_
        ln = lax.broadcasted_iota(jnp.int32, (w_, width), 1)
        if period is not None:
            ln = ln % period
        return (rh == ln // lanes_per_head).astype(BF16)

    bd_mask = blockdiag_mask(w_, c_)
    bdk_mask = blockdiag_mask(B_WIDTH, HEAD_DIM)
    bdr_mask = blockdiag_mask(2 * B_WIDTH, HEAD_DIM, period=B_WIDTH)

    dirs = []
    for d in range(N_DIR):
        diff = (ri - jj) if d == 0 else (jj - ri)
        d2 = (r2 - c2) if d == 0 else (c2 - r2)
        dirs.append(dict(incl=diff >= 0, strict=diff > 0, m2=(diff > 0).astype(F32),
                         m1=(d2 >= 0).astype(BF16)))
    outs = ((mqf_ref, bof_ref, dcf_ref), (mqb_ref, bob_ref, dcb_ref))
    row8 = lax.broadcasted_iota(jnp.int32, (SUBLANES, HEAD_DIM), 0)

    for g in range(gsz):
        sl = slice(g * c_, (g + 1) * c_)
        kc = k_ref[sl, :]
        qc = q_ref[sl, :]
        vc = v_ref[sl, :]
        bdk = _tile_rows(kc, h_) * bdk_mask
        kq = _dot(jnp.concatenate([kc, qc], axis=0), bdk, _NT)
        kk_w = kq[:c_]
        qk_w = kq[c_:]
        kf = kc.astype(F32)
        qf = qc.astype(F32)
        vf = vc.astype(F32)
        for d in range(N_DIR):
            m = dirs[d]
            mq_ref, bo_ref, dc_ref = outs[d]
            slot = g if d == 0 else gsz - 1 - g
            gt = g_ref[sl, d * HEAD_DIM:(d + 1) * HEAD_DIM]
            gl_w = _lane_blocks([gt[:, h:h + 1] for h in range(h_)], w_, c_)
            d3 = _dot(m["m1"], jnp.concatenate(_split3(gl_w * m["m2"]) + _split3(gt), axis=1))
            dm_w = d3[:, :w_] + d3[:, w_:2 * w_] + d3[:, 2 * w_:3 * w_]
            o3 = 3 * w_
            gcum = d3[:, o3:o3 + HEAD_DIM] + d3[:, o3 + HEAD_DIM:o3 + 2 * HEAD_DIM] + d3[:, o3 + 2 * HEAD_DIM:]
            decay = jnp.exp(dm_w)
            beta_w = _lane_blocks([gt[:, h_ + h:h_ + h + 1] for h in range(h_)], w_, c_)
            x = -jnp.where(m["strict"], kk_w * beta_w * decay, 0.0)
            xb = x.astype(BF16)
            xp = _dot(xb, _tile_rows(xb, h_) * bd_mask)
            p = eye_w + x
            for _ in range(4):
                xpb = xp.astype(BF16)
                r = _dot(jnp.concatenate([p.astype(BF16), xpb], axis=0), _tile_rows(xpb, h_) * bd_mask)
                p = p + r[:c_]
                xp = r[c_:]
            p = p + _dot(p.astype(BF16), _tile_rows(xp.astype(BF16), h_) * bd_mask)

            g_hd = _lane_blocks([gcum[:, h:h + 1] for h in range(h_)], B_WIDTH, HEAD_DIM)
            beta_hd = _lane_blocks([gt[:, h_ + h:h_ + h + 1] for h in range(h_)], B_WIDTH, HEAD_DIM)
            gtot = jnp.sum(gt, axis=0, keepdims=True)
            gtot_hd = _lane_blocks([gtot[:, h:h + 1] for h in range(h_)], B_WIDTH, HEAD_DIM)
            eg_hd = jnp.exp(g_hd)
            rhs = jnp.concatenate([vf * beta_hd, kf * (beta_hd * eg_hd)], axis=1).astype(BF16)
            uw = _dot(p.astype(BF16), _tile_rows(rhs, h_) * bdr_mask)
            intra = jnp.where(m["incl"], qk_w * decay, 0.0).astype(BF16)
            iuw = _dot(intra, _tile_rows(uw.astype(BF16), h_) * bdr_mask)
            o_local = iuw[:, :B_WIDTH]
            q_eff = qf * eg_hd - iuw[:, B_WIDTH:]
            k_tail = (kf * jnp.exp(gtot_hd - g_hd)).astype(BF16)
            dc_tile = jnp.zeros((SUBLANES, HEAD_DIM), F32)
            for h in range(h_):
                hs = slice(h * HEAD_DIM, (h + 1) * HEAD_DIM)
                wu = jnp.concatenate([-uw[:, B_WIDTH + h * HEAD_DIM:B_WIDTH + (h + 1) * HEAD_DIM], uw[:, hs]],
                                     axis=1).astype(BF16)
                mb = _dot(k_tail[:, hs], wu, _TN)
                mq_ref[slot, h, 0:HEAD_DIM, :] = mb[:, :HEAD_DIM].astype(BF16)
                mq_ref[slot, h, HEAD_DIM:, :] = q_eff[:, hs].astype(BF16)
                bo_ref[slot, h, 0:HEAD_DIM, :] = mb[:, HEAD_DIM:].astype(BF16)
                bo_ref[slot, h, HEAD_DIM:, :] = o_local[:, hs].astype(BF16)
                dc_tile = jnp.where(row8 == h, jnp.exp(gtot_hd[:, hs]), dc_tile)
            dc_ref[slot] = dc_tile


def _gdn_chunk(qkvn, gates, gsz):
    b, seq, _ = qkvn.shape
    nc = seq // CHUNK
    nb = nc // gsz
    rows = gsz * CHUNK
    mrows = HEAD_DIM + CHUNK
    fwd5 = lambda bi, i: (bi, i, 0, 0, 0)
    bwd5 = lambda bi, i: (bi, nb - 1 - i, 0, 0, 0)
    fwd4 = lambda bi, i: (bi, i, 0, 0)
    bwd4 = lambda bi, i: (bi, nb - 1 - i, 0, 0)
    big = jax.ShapeDtypeStruct((b, nc, B_HEADS, mrows, HEAD_DIM), BF16)
    small = jax.ShapeDtypeStruct((b, nc, SUBLANES, HEAD_DIM), F32)
    big_blk = (None, gsz, B_HEADS, mrows, HEAD_DIM)
    small_blk = (None, gsz, SUBLANES, HEAD_DIM)
    return pl.pallas_call(
        functools.partial(_gdn_chunk_body, gsz=gsz),
        grid=(b, nb),
        in_specs=[
            pl.BlockSpec((None, rows, B_WIDTH), lambda bi, i: (bi, i, 0)),
            pl.BlockSpec((None, rows, B_WIDTH), lambda bi, i: (bi, i, 1)),
            pl.BlockSpec((None, rows, B_WIDTH), lambda bi, i: (bi, i, 2)),
            pl.BlockSpec((None, rows, GATE_WIDTH), lambda bi, i: (bi, i, 0)),
        ],
        out_specs=[
            pl.BlockSpec(big_blk, fwd5), pl.BlockSpec(big_blk, fwd5), pl.BlockSpec(small_blk, fwd4),
            pl.BlockSpec(big_blk, bwd5), pl.BlockSpec(big_blk, bwd5), pl.BlockSpec(small_blk, bwd4),
        ],
        out_shape=[big, big, small, big, big, small],
        compiler_params=_params(("parallel", "parallel")),
        name="gdn_chunk",
    )(qkvn, qkvn, qkvn, gates)


def _gdn_scan_body(mqf_ref, bof_ref, dcf_ref, mqb_ref, bob_ref, dcb_ref, of_ref, ob_ref, s_ref, *, gsz):
    @pl.when(pl.program_id(1) == 0)
    def _():
        s_ref[...] = jnp.zeros_like(s_ref)

    ins = ((mqf_ref, bof_ref, dcf_ref, of_ref), (mqb_ref, bob_ref, dcb_ref, ob_ref))
    for g in range(gsz):
        for d in range(N_DIR):
            mq_ref, bo_ref, dc_ref, o_ref = ins[d]
            out_slot = g if d == 0 else gsz - 1 - g
            for h in range(B_HEADS):
                state = s_ref[d * B_HEADS + h]
                r = _dot(mq_ref[g, h], state.astype(BF16))
                bo = bo_ref[g, h].astype(F32)
                s_ref[d * B_HEADS + h] = state * dc_ref[g, h:h + 1, :] + r[:HEAD_DIM] + bo[:HEAD_DIM]
                o_ref[out_slot * CHUNK:(out_slot + 1) * CHUNK, h * HEAD_DIM:(h + 1) * HEAD_DIM] = (
                    r[HEAD_DIM:] + bo[HEAD_DIM:])


def _gdn_scan(ops, seq, gsz):
    mqf = ops[0]
    b, nc = mqf.shape[:2]
    nb = nc // gsz
    rows = gsz * CHUNK
    mrows = HEAD_DIM + CHUNK
    big_blk = (None, gsz, B_HEADS, mrows, HEAD_DIM)
    small_blk = (None, gsz, SUBLANES, HEAD_DIM)
    five = lambda bi, i: (bi, i, 0, 0, 0)
    four = lambda bi, i: (bi, i, 0, 0)
    out = jax.ShapeDtypeStruct((b, seq, B_WIDTH), F32)
    return pl.pallas_call(
        functools.partial(_gdn_scan_body, gsz=gsz),
        grid=(b, nb),
        in_specs=[pl.BlockSpec(big_blk, five), pl.BlockSpec(big_blk, five), pl.BlockSpec(small_blk, four)] * 2,
        out_specs=[
            pl.BlockSpec((None, rows, B_WIDTH), lambda bi, i: (bi, i, 0)),
            pl.BlockSpec((None, rows, B_WIDTH), lambda bi, i: (bi, nb - 1 - i, 0)),
        ],
        out_shape=[out, out],
        scratch_shapes=[pltpu.VMEM((N_DIR * B_HEADS, HEAD_DIM, HEAD_DIM), F32)],
        compiler_params=_params(("parallel", "arbitrary")),
        name="gdn_scan",
    )(*ops)


def _mem_kv_body(m_ref, nw_ref, w_ref, o_ref):
    x = m_ref[...]
    ms = jnp.mean(x * x, axis=-1, keepdims=True)
    h = (x * lax.rsqrt(ms + EPS) * nw_ref[...]).astype(BF16)
    o_ref[...] = _dot(h, w_ref[...]).astype(BF16)


def _mem_kv(mem2, nw, w, tm):
    rows = mem2.shape[0]
    return pl.pallas_call(
        _mem_kv_body,
        grid=(rows // tm,),
        in_specs=[
            pl.BlockSpec((tm, D_MODEL), lambda i: (i, 0)),
            pl.BlockSpec((1, D_MODEL), lambda i: (0, 0)),
            pl.BlockSpec((D_MODEL, 2 * M_WIDTH), lambda i: (0, 0)),
        ],
        out_specs=pl.BlockSpec((tm, 2 * M_WIDTH), lambda i: (i, 0)),
        out_shape=jax.ShapeDtypeStruct((rows, 2 * M_WIDTH), BF16),
        compiler_params=_params(("parallel",)),
        name="mem_kv",
    )(mem2, nw, w)


def _mem_attn_body(q_ref, kv_ref, z_ref, o_ref):
    for h in range(M_HEADS):
        hs = slice(h * HEAD_DIM, (h + 1) * HEAD_DIM)
        s = _dot(q_ref[:, hs], kv_ref[:, hs], _NT)
        m = jnp.max(s, axis=-1, keepdims=True)
        p = jnp.exp(s - m)
        l = jnp.sum(p, axis=-1, keepdims=True)
        vh = kv_ref[:, M_WIDTH + h * HEAD_DIM:M_WIDTH + (h + 1) * HEAD_DIM]
        y = _dot(p.astype(BF16), vh) / l
        o_ref[:, hs] = (y * _silu(z_ref[:, hs].astype(F32))).astype(BF16)


def _mem_attn(mq, mkv, z, tq):
    b, seq, _ = mq.shape
    n_mem = mkv.shape[1]
    z_blk = (A_WIDTH + B_WIDTH) // M_WIDTH
    return pl.pallas_call(
        _mem_attn_body,
        grid=(b, seq // tq),
        in_specs=[
            pl.BlockSpec((None, tq, M_WIDTH), lambda bi, i: (bi, i, 0)),
            pl.BlockSpec((None, n_mem, 2 * M_WIDTH), lambda bi, i: (bi, 0, 0)),
            pl.BlockSpec((None, tq, M_WIDTH), lambda bi, i: (bi, i, z_blk)),
        ],
        out_specs=pl.BlockSpec((None, tq, M_WIDTH), lambda bi, i: (bi, i, 0)),
        out_shape=jax.ShapeDtypeStruct((b, seq, M_WIDTH), BF16),
        compiler_params=_params(("parallel", "parallel")),
        name="mem_attn",
    )(mq, mkv, z)


def _out_proj_body(ya_ref, of_ref, ob_ref, zb_ref, ym_ref, dnw_ref, w_ref, pw_ref, x_ref, o_ref):
    ob = of_ref[...] + ob_ref[...]
    dnw = dnw_ref[...]
    parts = []
    for h in range(B_HEADS):
        hs = slice(h * HEAD_DIM, (h + 1) * HEAD_DIM)
        oh = ob[:, hs]
        yn = oh * lax.rsqrt(jnp.mean(oh * oh, axis=-1, keepdims=True) + EPS) * dnw
        parts.append((yn * _silu(zb_ref[:, hs].astype(F32))).astype(BF16))
    yb = jnp.concatenate(parts, axis=1)
    acc = _dot(ya_ref[...], w_ref[0:A_WIDTH, :])
    acc = acc + _dot(yb, w_ref[A_WIDTH:A_WIDTH + B_WIDTH, :])
    acc = acc + _dot(ym_ref[...], w_ref[A_WIDTH + B_WIDTH:, :])
    y = acc * lax.rsqrt(jnp.mean(acc * acc, axis=-1, keepdims=True) + EPS) * pw_ref[...]
    o_ref[...] = x_ref[...] + y


def _out_proj(ya, o_f, o_b, z, ym, dnw, w, pw, x, tm):
    b, seq, _ = ya.shape
    z_blk = A_WIDTH // B_WIDTH
    row = lambda bi, i: (bi, i, 0)
    const = lambda bi, i: (0, 0)
    return pl.pallas_call(
        _out_proj_body,
        grid=(b, seq // tm),
        in_specs=[
            pl.BlockSpec((None, tm, A_WIDTH), row),
            pl.BlockSpec((None, tm, B_WIDTH), row),
            pl.BlockSpec((None, tm, B_WIDTH), row),
            pl.BlockSpec((None, tm, B_WIDTH), lambda bi, i: (bi, i, z_blk)),
            pl.BlockSpec((None, tm, M_WIDTH), row),
            pl.BlockSpec((1, HEAD_DIM), const),
            pl.BlockSpec((D_MIX, D_MODEL), const),
            pl.BlockSpec((1, D_MODEL), const),
            pl.BlockSpec((None, tm, D_MODEL), row),
        ],
        out_specs=pl.BlockSpec((None, tm, D_MODEL), row),
        out_shape=jax.ShapeDtypeStruct((b, seq, D_MODEL), F32),
        compiler_params=_params(("parallel", "parallel")),
        name="out_proj",
    )(ya, o_f, o_b, z, ym, dnw, w, pw, x)


def _rope_tables(seq):
    t = np.arange(seq)
    inv_freq = ROPE_THETA ** (-jnp.arange(0, ROPE_AXIS_DIM, 2, dtype=F32) / ROPE_AXIS_DIM)
    ang_row = jnp.asarray(t // GRID_W, F32)[:, None] * inv_freq
    ang_col = jnp.asarray(t % GRID_W, F32)[:, None] * inv_freq
    cr, sr, cc, sc = jnp.cos(ang_row), jnp.sin(ang_row), jnp.cos(ang_col), jnp.sin(ang_col)
    return (jnp.concatenate([cr, cr, cc, cc], axis=-1),
            jnp.concatenate([-sr, sr, -sc, sc], axis=-1))


def _arrange_w_in(w):
    pts = np.cumsum([A_WIDTH, A_KV_WIDTH, A_KV_WIDTH, B_QKV_WIDTH, N_DIR * B_HEADS, N_DIR * B_HEADS, M_WIDTH])
    aq, ak, av, bqkv, ba, bb, mq, z = jnp.split(w, pts, axis=1)
    gate_blocks = []
    for d in range(N_DIR):
        sl = slice(d * B_HEADS, (d + 1) * B_HEADS)
        pad = jnp.zeros((w.shape[0], HEAD_DIM - 2 * B_HEADS), w.dtype)
        gate_blocks += [ba[:, sl], bb[:, sl], pad]
    return jnp.concatenate([aq, ak, av, bqkv] + gate_blocks + [mq, z], axis=1).astype(BF16)


def _gate_lane_vec(p):
    out = jnp.zeros((N_DIR, HEAD_DIM), F32).at[:, :B_HEADS].set(p.astype(F32))
    return out.reshape(1, GATE_WIDTH)


def kernel(x, mem, norm_pre_w, w_in, q_norm_w, k_norm_w, conv_w, a_log, dt_bias, delta_norm_w,
           mem_norm_w, w_mem_kv, w_out, norm_post_w):
    b, seq, _ = x.shape
    n_mem = mem.shape[1]
    assert seq % 512 == 0 and seq % GRID_W == 0 and w_in.shape[0] == 1
    l = 0
    tm = 512

    cos_t, sin_t = _rope_tables(seq)
    qkw = jnp.zeros((SUBLANES, HEAD_DIM), F32).at[0].set(q_norm_w[l]).at[1].set(k_norm_w[l])
    aq, ak, av, bqkv, gb, mq, z = _in_proj(
        x.reshape(b * seq, D_MODEL), norm_pre_w[l].reshape(1, D_MODEL), _arrange_w_in(w_in[l]),
        qkw, cos_t, sin_t, seq, tm)
    shp = lambda a: a.reshape(b, seq, a.shape[-1])
    ak, bqkv, gb, mq, z = map(shp, (ak, bqkv, gb, mq, z))

    ya = _attention(aq, ak, av, z, tq=512)

    cw = jnp.zeros((SUBLANES, B_QKV_WIDTH), F32).at[:CONV_K].set(conv_w[l])
    qkvn, gates = _gdn_prep(bqkv, cw, gb, _gate_lane_vec(a_log[l]), _gate_lane_vec(dt_bias[l]), tm=256)
    o_f, o_b = _gdn_scan(_gdn_chunk(qkvn, gates, gsz=4), seq, gsz=4)

    mkv = _mem_kv(mem.reshape(b * n_mem, D_MODEL), mem_norm_w[l].reshape(1, D_MODEL),
                  w_mem_kv[l].astype(BF16), tm=256)
    ym = _mem_attn(mq, mkv.reshape(b, n_mem, 2 * M_WIDTH), z, tq=512)

    return _out_proj(ya, o_f, o_b, z, ym, delta_norm_w[l].reshape(1, HEAD_DIM), w_out[l].astype(BF16),
                     norm_post_w[l].reshape(1, D_MODEL), x, tm=256)
```

```python
import functools

import jax
import jax.numpy as jnp
import numpy as np
from jax import lax
from jax.experimental import pallas as pl
from jax.experimental.pallas import tpu as pltpu

F32 = jnp.float32
BF16 = jnp.bfloat16

D_MODEL = 1024
HEAD_DIM = 128
GRID_W = 64
A_HEADS = 8
A_KV_HEADS = 2
A_GROUP = A_HEADS // A_KV_HEADS
A_WIDTH = A_HEADS * HEAD_DIM
A_KV_WIDTH = A_KV_HEADS * HEAD_DIM
B_HEADS = 4
B_WIDTH = B_HEADS * HEAD_DIM
B_QKV_WIDTH = 3 * B_WIDTH
N_DIR = 2
CONV_K = 5
CHUNK = 64
M_HEADS = 4
M_WIDTH = M_HEADS * HEAD_DIM
D_MIX = A_WIDTH + B_WIDTH + M_WIDTH
ROPE_AXIS_DIM = HEAD_DIM // 2
ROPE_THETA = 10000.0
EPS = 1e-6
LOG2E = 1.4426950408889634
GATE_WIDTH = N_DIR * HEAD_DIM
QK_WIDTH = A_WIDTH + A_KV_WIDTH
SUBLANES = 8
BF16_ROWS = 2 * SUBLANES
V_ROWS = HEAD_DIM + BF16_ROWS
VMEM_LIMIT = 56 * 1024 * 1024

_NT = (((1,), (1,)), ((), ()))
_TN = (((0,), (0,)), ((), ()))


def _dot(a, b, dims=None, precision=None):
    if dims is None:
        return jnp.dot(a, b, preferred_element_type=F32, precision=precision)
    return lax.dot_general(a, b, dims, preferred_element_type=F32, precision=precision)


def _silu(x):
    return x * jax.nn.sigmoid(x)


def _params(sem):
    return pltpu.CompilerParams(dimension_semantics=sem, vmem_limit_bytes=VMEM_LIMIT)


def _in_proj_body(x_ref, nw_ref, w_ref, qkw_ref, cos_ref, sin_ref,
                  aq_ref, ak_ref, av_ref, bqkv_ref, gb_ref, mq_ref, z_ref):
    x = x_ref[...]
    ms = jnp.mean(x * x, axis=-1, keepdims=True)
    h = (x * lax.rsqrt(ms + EPS) * nw_ref[...]).astype(BF16)

    cos = cos_ref[...]
    sin = sin_ref[...]
    lane = lax.broadcasted_iota(jnp.int32, cos.shape, 1)
    first_half = (lane % (ROPE_AXIS_DIM)) < (ROPE_AXIS_DIM // 2)

    def norm_rope(y, w):
        yn = y * lax.rsqrt(jnp.mean(y * y, axis=-1, keepdims=True) + EPS) * w
        swapped = jnp.where(first_half,
                            pltpu.roll(yn, HEAD_DIM - ROPE_AXIS_DIM // 2, 1),
                            pltpu.roll(yn, ROPE_AXIS_DIM // 2, 1))
        return yn * cos + swapped * sin

    col = 0
    for c0 in range(0, QK_WIDTH, 2 * HEAD_DIM):
        y = _dot(h, w_ref[:, c0:c0 + 2 * HEAD_DIM])
        for j in range(2):
            hc = c0 + j * HEAD_DIM
            yh = y[:, j * HEAD_DIM:(j + 1) * HEAD_DIM]
            if hc < A_WIDTH:
                out = norm_rope(yh, qkw_ref[0:1, :]) * (HEAD_DIM ** -0.5 * LOG2E)
                aq_ref[hc:hc + HEAD_DIM, :] = out.T.astype(BF16)
            else:
                out = norm_rope(yh, qkw_ref[1:2, :])
                ak_ref[:, hc - A_WIDTH:hc - A_WIDTH + HEAD_DIM] = out.astype(BF16)
    col = QK_WIDTH
    vt = _dot(h, w_ref[:, col:col + A_KV_WIDTH]).T.astype(BF16)
    pad_row = lax.broadcasted_iota(jnp.int32, (V_ROWS - HEAD_DIM, vt.shape[1]), 0)
    ones_pad = jnp.where(pad_row == 0, 1.0, 0.0).astype(BF16)
    for kv in range(A_KV_HEADS):
        av_ref[kv * V_ROWS:kv * V_ROWS + HEAD_DIM, :] = vt[kv * HEAD_DIM:(kv + 1) * HEAD_DIM]
        av_ref[kv * V_ROWS + HEAD_DIM:(kv + 1) * V_ROWS, :] = ones_pad
    col += A_KV_WIDTH
    for c0 in range(0, B_QKV_WIDTH, 512):
        bqkv_ref[:, c0:c0 + 512] = _dot(h, w_ref[:, col + c0:col + c0 + 512])
    col += B_QKV_WIDTH
    gb_ref[...] = _dot(h, w_ref[:, col:col + GATE_WIDTH])
    col += GATE_WIDTH
    mq_ref[...] = (_dot(h, w_ref[:, col:col + M_WIDTH]) * (HEAD_DIM ** -0.5)).astype(BF16)
    col += M_WIDTH
    for c0 in range(0, D_MIX, 512):
        z_ref[:, c0:c0 + 512] = _dot(h, w_ref[:, col + c0:col + c0 + 512]).astype(BF16)


def _in_proj(x2, nw, w_all, qkw, cos_t, sin_t, seq, tm):
    rows = x2.shape[0]
    n_seq_tiles = seq // tm
    wtot = w_all.shape[1]
    row = lambda i: (i, 0)
    const = lambda i: (0, 0)
    nb = rows // seq
    outs = [
        (A_KV_WIDTH, BF16), (B_QKV_WIDTH, F32), (GATE_WIDTH, F32), (M_WIDTH, BF16), (D_MIX, BF16),
    ]
    t_specs = [
        pl.BlockSpec((None, None, A_WIDTH, tm), lambda i: (i // n_seq_tiles, i % n_seq_tiles, 0, 0)),
        pl.BlockSpec((None, None, A_KV_HEADS * V_ROWS, tm), lambda i: (i // n_seq_tiles, i % n_seq_tiles, 0, 0)),
    ]
    t_shapes = [
        jax.ShapeDtypeStruct((nb, n_seq_tiles, A_WIDTH, tm), BF16),
        jax.ShapeDtypeStruct((nb, n_seq_tiles, A_KV_HEADS * V_ROWS, tm), BF16),
    ]
    return pl.pallas_call(
        _in_proj_body,
        grid=(rows // tm,),
        in_specs=[
            pl.BlockSpec((tm, D_MODEL), row),
            pl.BlockSpec((1, D_MODEL), const),
            pl.BlockSpec((D_MODEL, wtot), const, pipeline_mode=pl.Buffered(1)),
            pl.BlockSpec((SUBLANES, HEAD_DIM), const),
            pl.BlockSpec((tm, HEAD_DIM), lambda i: (i % n_seq_tiles, 0)),
            pl.BlockSpec((tm, HEAD_DIM), lambda i: (i % n_seq_tiles, 0)),
        ],
        out_specs=[t_specs[0], pl.BlockSpec((tm, outs[0][0]), row), t_specs[1]]
        + [pl.BlockSpec((tm, w), row) for w, _ in outs[1:]],
        out_shape=[t_shapes[0], jax.ShapeDtypeStruct((rows, outs[0][0]), outs[0][1]), t_shapes[1]]
        + [jax.ShapeDtypeStruct((rows, w), dt) for w, dt in outs[1:]],
        compiler_params=_params(("parallel",)),
        name="in_proj",
    )(x2, nw, w_all, qkw, cos_t, sin_t)


def _attn_body(qt_ref, k_ref, vt_ref, z_ref, o_ref, m_ref, acc_ref, s_ref, p_ref, *, n_kv, n_q, tq):
    def scores(qi, j, slot):
        kt = k_ref[j]
        for g in range(A_GROUP):
            s_ref[slot, g] = _dot(kt, qt_ref[qi, g * HEAD_DIM:(g + 1) * HEAD_DIM, :])

    tk = s_ref.shape[2]
    sub = SUBLANES
    rc = 4 * sub

    def softmax_pv(j, slot):
        vt = vt_ref[j]
        for g in range(A_GROUP):
            mx = s_ref[slot, g, 0:sub, :]
            for c in range(1, tk // sub):
                mx = jnp.maximum(mx, s_ref[slot, g, c * sub:(c + 1) * sub, :])
            m_new = jnp.max(mx, axis=0, keepdims=True)
            if j > 0:
                m_old = m_ref[g]
                m_new = jnp.maximum(m_old, m_new)
                alpha = jnp.exp2(m_old - m_new)
            m_ref[g] = m_new
            for c in range(tk // rc):
                pc = jnp.exp2(s_ref[slot, g, c * rc:(c + 1) * rc, :] - m_new)
                p_ref[slot, g, c * rc:(c + 1) * rc, :] = pc.astype(BF16)
            pv = _dot(vt, p_ref[slot, g])
            if j == 0:
                acc_ref[g] = pv
            else:
                acc_ref[g] = alpha * acc_ref[g] + pv

    scores(0, 0, 0)

    def q_tile(qi, carry):
        for j in range(n_kv):
            slot = j % 2
            if j + 1 < n_kv:
                scores(qi, j + 1, 1 - slot)
            else:
                scores(jnp.minimum(qi + 1, n_q - 1), 0, 1 - slot)
            softmax_pv(j, slot)
        rows = pl.ds(pl.multiple_of(qi * tq, tq), tq)
        for g in range(A_GROUP):
            hs = slice(g * HEAD_DIM, (g + 1) * HEAD_DIM)
            y = (acc_ref[g, 0:HEAD_DIM, :] / acc_ref[g, HEAD_DIM:HEAD_DIM + 1, :]).T
            o_ref[rows, hs] = (y * _silu(z_ref[rows, hs].astype(F32))).astype(BF16)
        return carry

    lax.fori_loop(0, n_q, q_tile, 0)


def _attention(aqt, ak, avt, z):
    b, n_q, _, tq = aqt.shape
    n_kv, tk = avt.shape[1], avt.shape[3]
    seq = n_kv * tk
    assert n_kv % 2 == 0
    gw = A_GROUP * HEAD_DIM
    akr = ak.reshape(b, n_kv, tk, A_KV_WIDTH)
    return pl.pallas_call(
        functools.partial(_attn_body, n_kv=n_kv, n_q=n_q, tq=tq),
        grid=(b, A_KV_HEADS),
        in_specs=[
            pl.BlockSpec((None, n_q, gw, tq), lambda bi, kv: (bi, 0, kv, 0)),
            pl.BlockSpec((None, n_kv, tk, HEAD_DIM), lambda bi, kv: (bi, 0, 0, kv)),
            pl.BlockSpec((None, n_kv, V_ROWS, tk), lambda bi, kv: (bi, 0, kv, 0)),
            pl.BlockSpec((None, seq, gw), lambda bi, kv: (bi, 0, kv)),
        ],
        out_specs=pl.BlockSpec((None, seq, gw), lambda bi, kv: (bi, 0, kv)),
        out_shape=jax.ShapeDtypeStruct((b, seq, A_WIDTH), BF16),
        scratch_shapes=[pltpu.VMEM((A_GROUP, 1, tq), F32), pltpu.VMEM((A_GROUP, V_ROWS, tq), F32), pltpu.VMEM((2, A_GROUP, tk, tq), F32),
                        pltpu.VMEM((2, A_GROUP, tk, tq), BF16)],
        compiler_params=_params(("parallel", "arbitrary")),
        name="attn",
    )(aqt, akr, avt, z)


def _gdn_prep_body(cur_ref, prev_ref, next_ref, cw_ref, gb_ref, alog_ref, dtb_ref,
                   qkv_ref, gate_ref, ext_ref, *, tm, n_tiles):
    i = pl.program_id(1)
    halo = SUBLANES
    prev = prev_ref[...]
    nxt = next_ref[...]
    ext_ref[0:halo, :] = jnp.where(i == 0, jnp.zeros_like(prev), prev)
    ext_ref[halo:halo + tm, :] = cur_ref[...]
    ext_ref[halo + tm:, :] = jnp.where(i == n_tiles - 1, jnp.zeros_like(nxt), nxt)

    y = None
    for j in range(CONV_K):
        term = ext_ref[pl.ds(halo - CONV_K // 2 + j, tm), :] * cw_ref[j:j + 1, :]
        y = term if y is None else y + term
    y = _silu(y)

    for hh in range(2 * B_HEADS):
        c0 = hh * HEAD_DIM
        yh = y[:, c0:c0 + HEAD_DIM]
        yn = yh * lax.rsqrt(jnp.sum(yh * yh, axis=-1, keepdims=True) + EPS)
        if hh < B_HEADS:
            yn = yn * (HEAD_DIM ** -0.5)
        qkv_ref[:, c0:c0 + HEAD_DIM] = yn.astype(BF16)
    qkv_ref[:, 2 * B_WIDTH:] = y[:, 2 * B_WIDTH:].astype(BF16)

    g = gb_ref[...]
    lane = lax.broadcasted_iota(jnp.int32, g.shape, 1) % HEAD_DIM
    t = g + dtb_ref[...]
    softplus = jnp.maximum(t, 0.0) + jnp.log1p(jnp.exp(-jnp.abs(t)))
    glog = -jnp.exp(alog_ref[...]) * softplus
    beta = jax.nn.sigmoid(g)
    gate_ref[...] = jnp.where(lane < B_HEADS, glog, jnp.where(lane < 2 * B_HEADS, beta, 0.0))


def _gdn_prep(bqkv, cw, gb, alog_v, dtb_v, tm):
    b, seq, _ = bqkv.shape
    n_tiles = seq // tm
    per = tm // SUBLANES
    last_blk = seq // SUBLANES - 1
    return pl.pallas_call(
        functools.partial(_gdn_prep_body, tm=tm, n_tiles=n_tiles),
        grid=(b, n_tiles),
        in_specs=[
            pl.BlockSpec((None, tm, B_QKV_WIDTH), lambda bi, i: (bi, i, 0)),
            pl.BlockSpec((None, SUBLANES, B_QKV_WIDTH), lambda bi, i: (bi, jnp.maximum(i * per - 1, 0), 0)),
            pl.BlockSpec((None, SUBLANES, B_QKV_WIDTH), lambda bi, i: (bi, jnp.minimum((i + 1) * per, last_blk), 0)),
            pl.BlockSpec((SUBLANES, B_QKV_WIDTH), lambda bi, i: (0, 0)),
            pl.BlockSpec((None, tm, GATE_WIDTH), lambda bi, i: (bi, i, 0)),
            pl.BlockSpec((1, GATE_WIDTH), lambda bi, i: (0, 0)),
            pl.BlockSpec((1, GATE_WIDTH), lambda bi, i: (0, 0)),
        ],
        out_specs=[
            pl.BlockSpec((None, tm, B_QKV_WIDTH), lambda bi, i: (bi, i, 0)),
            pl.BlockSpec((None, tm, GATE_WIDTH), lambda bi, i: (bi, i, 0)),
        ],
        out_shape=[
            jax.ShapeDtypeStruct((b, seq, B_QKV_WIDTH), BF16),
            jax.ShapeDtypeStruct((b, seq, GATE_WIDTH), F32),
        ],
        scratch_shapes=[pltpu.VMEM((tm + 2 * SUBLANES, B_QKV_WIDTH), F32)],
        compiler_params=_params(("parallel", "parallel")),
        name="gdn_prep",
    )(bqkv, bqkv, bqkv, cw, gb, alog_v, dtb_v)


def _split3(a):
    hi = a.astype(BF16)
    r1 = a - hi.astype(F32)
    mid = r1.astype(BF16)
    lo = (r1 - mid.astype(F32)).astype(BF16)
    return [hi, mid, lo]


def _lane_blocks(cols, width, block):
    rows = cols[0].shape[0]
    blk = lax.broadcasted_iota(jnp.int32, (rows, width), 1) // block
    out = jnp.broadcast_to(cols[-1], (rows, width))
    for i in range(len(cols) - 2, -1, -1):
        out = jnp.where(blk == i, cols[i], out)
    return out


def _tile_rows(x, n):
    return jnp.concatenate([x] * n, axis=0)


def _gdn_chunk_body(q_ref, k_ref, v_ref, g_ref, mqf_ref, bof_ref, dcf_ref, mqb_ref, bob_ref, dcb_ref, *, gsz):
    c_, h_, w_ = CHUNK, B_HEADS, B_HEADS * CHUNK
    ri = lax.broadcasted_iota(jnp.int32, (c_, w_), 0)
    jj = lax.broadcasted_iota(jnp.int32, (c_, w_), 1) % c_
    eye_w = (ri == jj).astype(F32)
    r2 = lax.broadcasted_iota(jnp.int32, (c_, c_), 0)
    c2 = lax.broadcasted_iota(jnp.int32, (c_, c_), 1)

    def blockdiag_mask(width, lanes_per_head, period=None):
        rh = lax.broadcasted_iota(jnp.int32, (w_, width), 0) // c_
        ln = lax.broadcasted_iota(jnp.int32, (w_, width), 1)
        if period is not None:
            ln = ln % period
        return (rh == ln // lanes_per_head).astype(BF16)

    bd_mask = blockdiag_mask(w_, c_)
    bdk_mask = blockdiag_mask(B_WIDTH, HEAD_DIM)
    bdr_mask = blockdiag_mask(2 * B_WIDTH, HEAD_DIM, period=B_WIDTH)

    dirs = []
    for d in range(N_DIR):
        diff = (ri - jj) if d == 0 else (jj - ri)
        d2 = (r2 - c2) if d == 0 else (c2 - r2)
        dirs.append(dict(incl=diff >= 0, strict=diff > 0, m2=(diff > 0).astype(F32),
                         m1=(d2 >= 0).astype(BF16)))
    outs = ((mqf_ref, bof_ref, dcf_ref), (mqb_ref, bob_ref, dcb_ref))
    row8 = lax.broadcasted_iota(jnp.int32, (SUBLANES, HEAD_DIM), 0)

    chunks = []
    for g in range(gsz):
        sl = slice(g * c_, (g + 1) * c_)
        kc = k_ref[sl, :]
        qc = q_ref[sl, :]
        kq = _dot(jnp.concatenate([kc, qc], axis=0), _tile_rows(kc, h_) * bdk_mask, _NT)
        chunks.append(dict(sl=sl, kk_w=kq[:c_], qk_w=kq[c_:]))

    inst = [dict(g=g, d=d, m=dirs[d], ck=chunks[g]) for g in range(gsz) for d in range(N_DIR)]
    for t in inst:
        m = t["m"]
        gt = g_ref[t["ck"]["sl"], t["d"] * HEAD_DIM:(t["d"] + 1) * HEAD_DIM]
        gl_w = _lane_blocks([gt[:, h:h + 1] for h in range(h_)], w_, c_)
        d3 = _dot(m["m1"], jnp.concatenate(_split3(gl_w * m["m2"]) + _split3(gt), axis=1))
        dm_w = d3[:, :w_] + d3[:, w_:2 * w_] + d3[:, 2 * w_:3 * w_]
        o3 = 3 * w_
        t["gcum"] = d3[:, o3:o3 + HEAD_DIM] + d3[:, o3 + HEAD_DIM:o3 + 2 * HEAD_DIM] + d3[:, o3 + 2 * HEAD_DIM:]
        t["gt"] = gt
        t["decay"] = jnp.exp(dm_w)
        beta_w = _lane_blocks([gt[:, h_ + h:h_ + h + 1] for h in range(h_)], w_, c_)
        t["x"] = -jnp.where(m["strict"], t["ck"]["kk_w"] * beta_w * t["decay"], 0.0)

    for t in inst:
        xb = t["x"].astype(BF16)
        t["xp"] = _dot(xb, _tile_rows(xb, h_) * bd_mask)
        t["p"] = eye_w + t["x"]
    for _ in range(4):
        for t in inst:
            xpb = t["xp"].astype(BF16)
            r = _dot(jnp.concatenate([t["p"].astype(BF16), xpb], axis=0), _tile_rows(xpb, h_) * bd_mask)
            t["p"] = t["p"] + r[:c_]
            t["xp"] = r[c_:]
    for t in inst:
        t["p"] = t["p"] + _dot(t["p"].astype(BF16), _tile_rows(t["xp"].astype(BF16), h_) * bd_mask)

    for t in inst:
        gt, sl = t["gt"], t["ck"]["sl"]
        kf = k_ref[sl, :].astype(F32)
        vf = v_ref[sl, :].astype(F32)
        g_hd = _lane_blocks([t["gcum"][:, h:h + 1] for h in range(h_)], B_WIDTH, HEAD_DIM)
        beta_hd = _lane_blocks([gt[:, h_ + h:h_ + h + 1] for h in range(h_)], B_WIDTH, HEAD_DIM)
        gtot = jnp.sum(gt, axis=0, keepdims=True)
        gtot_hd = _lane_blocks([gtot[:, h:h + 1] for h in range(h_)], B_WIDTH, HEAD_DIM)
        t["eg_hd"] = jnp.exp(g_hd)
        t["dc_hd"] = jnp.exp(gtot_hd)
        t["k_tail"] = (kf * jnp.exp(gtot_hd - g_hd)).astype(BF16)
        rhs = jnp.concatenate([vf * beta_hd, kf * (beta_hd * t["eg_hd"])], axis=1).astype(BF16)
        t["uw"] = _dot(t["p"].astype(BF16), _tile_rows(rhs, h_) * bdr_mask)
    for t in inst:
        intra = jnp.where(t["m"]["incl"], t["ck"]["qk_w"] * t["decay"], 0.0).astype(BF16)
        t["iuw"] = _dot(intra, _tile_rows(t["uw"].astype(BF16), h_) * bdr_mask)
    for t in inst:
        uw = t["uw"]
        t["mb"] = []
        for h in range(h_):
            hs = slice(h * HEAD_DIM, (h + 1) * HEAD_DIM)
            wu = jnp.concatenate([-uw[:, B_WIDTH + h * HEAD_DIM:B_WIDTH + (h + 1) * HEAD_DIM], uw[:, hs]],
                                 axis=1).astype(BF16)
            t["mb"].append(_dot(t["k_tail"][:, hs], wu, _TN))
    for t in inst:
        mq_ref, bo_ref, dc_ref = outs[t["d"]]
        slot = t["g"] if t["d"] == 0 else gsz - 1 - t["g"]
        qf = q_ref[t["ck"]["sl"], :].astype(F32)
        o_local = t["iuw"][:, :B_WIDTH]
        q_eff = qf * t["eg_hd"] - t["iuw"][:, B_WIDTH:]
        dc_tile = jnp.zeros((SUBLANES, HEAD_DIM), F32)
        for h in range(h_):
            hs = slice(h * HEAD_DIM, (h + 1) * HEAD_DIM)
            mb = t["mb"][h]
            mq_ref[slot, h, 0:HEAD_DIM, :] = mb[:, :HEAD_DIM].astype(BF16)
            mq_ref[slot, h, HEAD_DIM:, :] = q_eff[:, hs].astype(BF16)
            bo_ref[slot, h, 0:HEAD_DIM, :] = mb[:, HEAD_DIM:].astype(BF16)
            bo_ref[slot, h, HEAD_DIM:, :] = o_local[:, hs].astype(BF16)
            dc_tile = jnp.where(row8 == h, t["dc_hd"][:, hs], dc_tile)
        dc_ref[slot] = dc_tile


def _gdn_chunk(qkvn, gates, gsz):
    b, seq, _ = qkvn.shape
    nc = seq // CHUNK
    nb = nc // gsz
    rows = gsz * CHUNK
    mrows = HEAD_DIM + CHUNK
    fwd5 = lambda bi, i: (bi, i, 0, 0, 0)
    bwd5 = lambda bi, i: (bi, nb - 1 - i, 0, 0, 0)
    fwd4 = lambda bi, i: (bi, i, 0, 0)
    bwd4 = lambda bi, i: (bi, nb - 1 - i, 0, 0)
    big = jax.ShapeDtypeStruct((b, nc, B_HEADS, mrows, HEAD_DIM), BF16)
    small = jax.ShapeDtypeStruct((b, nc, SUBLANES, HEAD_DIM), F32)
    big_blk = (None, gsz, B_HEADS, mrows, HEAD_DIM)
    small_blk = (None, gsz, SUBLANES, HEAD_DIM)
    return pl.pallas_call(
        functools.partial(_gdn_chunk_body, gsz=gsz),
        grid=(b, nb),
        in_specs=[
            pl.BlockSpec((None, rows, B_WIDTH), lambda bi, i: (bi, i, 0)),
            pl.BlockSpec((None, rows, B_WIDTH), lambda bi, i: (bi, i, 1)),
            pl.BlockSpec((None, rows, B_WIDTH), lambda bi, i: (bi, i, 2)),
            pl.BlockSpec((None, rows, GATE_WIDTH), lambda bi, i: (bi, i, 0)),
        ],
        out_specs=[
            pl.BlockSpec(big_blk, fwd5), pl.BlockSpec(big_blk, fwd5), pl.BlockSpec(small_blk, fwd4),
            pl.BlockSpec(big_blk, bwd5), pl.BlockSpec(big_blk, bwd5), pl.BlockSpec(small_blk, bwd4),
        ],
        out_shape=[big, big, small, big, big, small],
        compiler_params=_params(("parallel", "parallel")),
        name="gdn_chunk",
    )(qkvn, qkvn, qkvn, gates)


def _gdn_scan_body(mqf_ref, bof_ref, dcf_ref, mqb_ref, bob_ref, dcb_ref, of_ref, ob_ref, s_ref, *, gsz):
    @pl.when(pl.program_id(1) == 0)
    def _():
        s_ref[...] = jnp.zeros_like(s_ref)

    ins = ((mqf_ref, bof_ref, dcf_ref, of_ref), (mqb_ref, bob_ref, dcb_ref, ob_ref))
    for g in range(gsz):
        for d in range(N_DIR):
            mq_ref, bo_ref, dc_ref, o_ref = ins[d]
            out_slot = g if d == 0 else gsz - 1 - g
            for h in range(B_HEADS):
                state = s_ref[d * B_HEADS + h]
                r = _dot(mq_ref[g, h], state.astype(BF16))
                bo = bo_ref[g, h].astype(F32)
                s_ref[d * B_HEADS + h] = state * dc_ref[g, h:h + 1, :] + r[:HEAD_DIM] + bo[:HEAD_DIM]
                o_ref[out_slot * CHUNK:(out_slot + 1) * CHUNK, h * HEAD_DIM:(h + 1) * HEAD_DIM] = (
                    r[HEAD_DIM:] + bo[HEAD_DIM:]).astype(BF16)


def _gdn_scan(ops, seq, gsz):
    mqf = ops[0]
    b, nc = mqf.shape[:2]
    nb = nc // gsz
    rows = gsz * CHUNK
    mrows = HEAD_DIM + CHUNK
    big_blk = (None, gsz, B_HEADS, mrows, HEAD_DIM)
    small_blk = (None, gsz, SUBLANES, HEAD_DIM)
    five = lambda bi, i: (bi, i, 0, 0, 0)
    four = lambda bi, i: (bi, i, 0, 0)
    out = jax.ShapeDtypeStruct((b, seq, B_WIDTH), BF16)
    return pl.pallas_call(
        functools.partial(_gdn_scan_body, gsz=gsz),
        grid=(b, nb),
        in_specs=[pl.BlockSpec(big_blk, five), pl.BlockSpec(big_blk, five), pl.BlockSpec(small_blk, four)] * 2,
        out_specs=[
            pl.BlockSpec((None, rows, B_WIDTH), lambda bi, i: (bi, i, 0)),
            pl.BlockSpec((None, rows, B_WIDTH), lambda bi, i: (bi, nb - 1 - i, 0)),
        ],
        out_shape=[out, out],
        scratch_shapes=[pltpu.VMEM((N_DIR * B_HEADS, HEAD_DIM, HEAD_DIM), F32)],
        compiler_params=_params(("parallel", "arbitrary")),
        name="gdn_scan",
    )(*ops)


def _mem_kv_body(m_ref, nw_ref, w_ref, o_ref):
    x = m_ref[...]
    ms = jnp.mean(x * x, axis=-1, keepdims=True)
    h = (x * lax.rsqrt(ms + EPS) * nw_ref[...]).astype(BF16)
    o_ref[...] = _dot(h, w_ref[...]).astype(BF16)


def _mem_kv(mem2, nw, w, tm):
    rows = mem2.shape[0]
    return pl.pallas_call(
        _mem_kv_body,
        grid=(rows // tm,),
        in_specs=[
            pl.BlockSpec((tm, D_MODEL), lambda i: (i, 0)),
            pl.BlockSpec((1, D_MODEL), lambda i: (0, 0)),
            pl.BlockSpec((D_MODEL, 2 * M_WIDTH), lambda i: (0, 0)),
        ],
        out_specs=pl.BlockSpec((tm, 2 * M_WIDTH), lambda i: (i, 0)),
        out_shape=jax.ShapeDtypeStruct((rows, 2 * M_WIDTH), BF16),
        compiler_params=_params(("parallel",)),
        name="mem_kv",
    )(mem2, nw, w)


def _mem_attn_body(q_ref, kv_ref, z_ref, o_ref):
    for h in range(M_HEADS):
        hs = slice(h * HEAD_DIM, (h + 1) * HEAD_DIM)
        s = _dot(q_ref[:, hs], kv_ref[:, hs], _NT)
        m = jnp.max(s, axis=-1, keepdims=True)
        p = jnp.exp(s - m)
        l = jnp.sum(p, axis=-1, keepdims=True)
        vh = kv_ref[:, M_WIDTH + h * HEAD_DIM:M_WIDTH + (h + 1) * HEAD_DIM]
        y = _dot(p.astype(BF16), vh) / l
        o_ref[:, hs] = (y * _silu(z_ref[:, hs].astype(F32))).astype(BF16)


def _mem_attn(mq, mkv, z, tq):
    b, seq, _ = mq.shape
    n_mem = mkv.shape[1]
    z_blk = (A_WIDTH + B_WIDTH) // M_WIDTH
    return pl.pallas_call(
        _mem_attn_body,
        grid=(b, seq // tq),
        in_specs=[
            pl.BlockSpec((None, tq, M_WIDTH), lambda bi, i: (bi, i, 0)),
            pl.BlockSpec((None, n_mem, 2 * M_WIDTH), lambda bi, i: (bi, 0, 0)),
            pl.BlockSpec((None, tq, M_WIDTH), lambda bi, i: (bi, i, z_blk)),
        ],
        out_specs=pl.BlockSpec((None, tq, M_WIDTH), lambda bi, i: (bi, i, 0)),
        out_shape=jax.ShapeDtypeStruct((b, seq, M_WIDTH), BF16),
        compiler_params=_params(("parallel", "parallel")),
        name="mem_attn",
    )(mq, mkv, z)


def _out_proj_body(ya_ref, of_ref, ob_ref, zb_ref, ym_ref, dnw_ref, w_ref, pw_ref, x_ref, o_ref):
    ob = of_ref[...].astype(F32) + ob_ref[...].astype(F32)
    dnw = dnw_ref[...]
    parts = []
    for h in range(B_HEADS):
        hs = slice(h * HEAD_DIM, (h + 1) * HEAD_DIM)
        oh = ob[:, hs]
        yn = oh * lax.rsqrt(jnp.mean(oh * oh, axis=-1, keepdims=True) + EPS) * dnw
        parts.append((yn * _silu(zb_ref[:, hs].astype(F32))).astype(BF16))
    yb = jnp.concatenate(parts, axis=1)
    acc = _dot(ya_ref[...], w_ref[0:A_WIDTH, :])
    acc = acc + _dot(yb, w_ref[A_WIDTH:A_WIDTH + B_WIDTH, :])
    acc = acc + _dot(ym_ref[...], w_ref[A_WIDTH + B_WIDTH:, :])
    y = acc * lax.rsqrt(jnp.mean(acc * acc, axis=-1, keepdims=True) + EPS) * pw_ref[...]
    o_ref[...] = x_ref[...] + y


def _out_proj(ya, o_f, o_b, z, ym, dnw, w, pw, x, tm):
    b, seq, _ = ya.shape
    z_blk = A_WIDTH // B_WIDTH
    row = lambda bi, i: (bi, i, 0)
    const = lambda bi, i: (0, 0)
    return pl.pallas_call(
        _out_proj_body,
        grid=(b, seq // tm),
        in_specs=[
            pl.BlockSpec((None, tm, A_WIDTH), row),
            pl.BlockSpec((None, tm, B_WIDTH), row),
            pl.BlockSpec((None, tm, B_WIDTH), row),
            pl.BlockSpec((None, tm, B_WIDTH), lambda bi, i: (bi, i, z_blk)),
            pl.BlockSpec((None, tm, M_WIDTH), row),
            pl.BlockSpec((1, HEAD_DIM), const),
            pl.BlockSpec((D_MIX, D_MODEL), const),
            pl.BlockSpec((1, D_MODEL), const),
            pl.BlockSpec((None, tm, D_MODEL), row),
        ],
        out_specs=pl.BlockSpec((None, tm, D_MODEL), row),
        out_shape=jax.ShapeDtypeStruct((b, seq, D_MODEL), F32),
        compiler_params=_params(("parallel", "parallel")),
        name="out_proj",
    )(ya, o_f, o_b, z, ym, dnw, w, pw, x)


def _rope_tables(seq):
    t = np.arange(seq)
    inv_freq = ROPE_THETA ** (-jnp.arange(0, ROPE_AXIS_DIM, 2, dtype=F32) / ROPE_AXIS_DIM)
    ang_row = jnp.asarray(t // GRID_W, F32)[:, None] * inv_freq
    ang_col = jnp.asarray(t % GRID_W, F32)[:, None] * inv_freq
    cr, sr, cc, sc = jnp.cos(ang_row), jnp.sin(ang_row), jnp.cos(ang_col), jnp.sin(ang_col)
    return (jnp.concatenate([cr, cr, cc, cc], axis=-1),
            jnp.concatenate([-sr, sr, -sc, sc], axis=-1))


def _arrange_w_in(w):
    pts = np.cumsum([A_WIDTH, A_KV_WIDTH, A_KV_WIDTH, B_QKV_WIDTH, N_DIR * B_HEADS, N_DIR * B_HEADS, M_WIDTH])
    aq, ak, av, bqkv, ba, bb, mq, z = jnp.split(w, pts, axis=1)
    gate_blocks = []
    for d in range(N_DIR):
        sl = slice(d * B_HEADS, (d + 1) * B_HEADS)
        pad = jnp.zeros((w.shape[0], HEAD_DIM - 2 * B_HEADS), w.dtype)
        gate_blocks += [ba[:, sl], bb[:, sl], pad]
    return jnp.concatenate([aq, ak, av, bqkv] + gate_blocks + [mq, z], axis=1).astype(BF16)


def _gate_lane_vec(p):
    out = jnp.zeros((N_DIR, HEAD_DIM), F32).at[:, :B_HEADS].set(p.astype(F32))
    return out.reshape(1, GATE_WIDTH)


def kernel(x, mem, norm_pre_w, w_in, q_norm_w, k_norm_w, conv_w, a_log, dt_bias, delta_norm_w,
           mem_norm_w, w_mem_kv, w_out, norm_post_w):
    b, seq, _ = x.shape
    n_mem = mem.shape[1]
    assert seq % 512 == 0 and seq % GRID_W == 0 and w_in.shape[0] == 1
    l = 0
    tm = 512

    cos_t, sin_t = _rope_tables(seq)
    qkw = jnp.zeros((SUBLANES, HEAD_DIM), F32).at[0].set(q_norm_w[l]).at[1].set(k_norm_w[l])
    aq, ak, av, bqkv, gb, mq, z = _in_proj(
        x.reshape(b * seq, D_MODEL), norm_pre_w[l].reshape(1, D_MODEL), _arrange_w_in(w_in[l]),
        qkw, cos_t, sin_t, seq, tm)
    shp = lambda a: a.reshape(b, seq, a.shape[-1])
    ak, bqkv, gb, mq, z = map(shp, (ak, bqkv, gb, mq, z))

    ya = _attention(aq, ak, av, z)

    cw = jnp.zeros((SUBLANES, B_QKV_WIDTH), F32).at[:CONV_K].set(conv_w[l])
    qkvn, gates = _gdn_prep(bqkv, cw, gb, _gate_lane_vec(a_log[l]), _gate_lane_vec(dt_bias[l]), tm=256)
    o_f, o_b = _gdn_scan(_gdn_chunk(qkvn, gates, gsz=4), seq, gsz=4)

    mkv = _mem_kv(mem.reshape(b * n_mem, D_MODEL), mem_norm_w[l].reshape(1, D_MODEL),
                  w_mem_kv[l].astype(BF16), tm=256)
    ym = _mem_attn(mq, mkv.reshape(b, n_mem, 2 * M_WIDTH), z, tq=512)

    return _out_proj(ya, o_f, o_b, z, ym, delta_norm_w[l].reshape(1, HEAD_DIM), w_out[l].astype(BF16),
                     norm_post_w[l].reshape(1, D_MODEL), x, tm=512)
```

```python
import functools

import jax
import jax.numpy as jnp
import numpy as np
from jax import lax
from jax.experimental import pallas as pl
from jax.experimental.pallas import tpu as pltpu

F32 = jnp.float32
BF16 = jnp.bfloat16

D_MODEL = 1024
HEAD_DIM = 128
GRID_W = 64
A_HEADS = 8
A_KV_HEADS = 2
A_GROUP = A_HEADS // A_KV_HEADS
A_WIDTH = A_HEADS * HEAD_DIM
A_KV_WIDTH = A_KV_HEADS * HEAD_DIM
B_HEADS = 4
B_WIDTH = B_HEADS * HEAD_DIM
B_QKV_WIDTH = 3 * B_WIDTH
N_DIR = 2
CONV_K = 5
CHUNK = 64
M_HEADS = 4
M_WIDTH = M_HEADS * HEAD_DIM
D_MIX = A_WIDTH + B_WIDTH + M_WIDTH
ROPE_AXIS_DIM = HEAD_DIM // 2
ROPE_THETA = 10000.0
EPS = 1e-6
LOG2E = 1.4426950408889634
GATE_WIDTH = N_DIR * HEAD_DIM
QK_WIDTH = A_WIDTH + A_KV_WIDTH
SUBLANES = 8
BF16_ROWS = 2 * SUBLANES
V_ROWS = HEAD_DIM + BF16_ROWS
VMEM_LIMIT = 56 * 1024 * 1024

_NT = (((1,), (1,)), ((), ()))
_TN = (((0,), (0,)), ((), ()))


def _dot(a, b, dims=None, precision=None):
    if dims is None:
        return jnp.dot(a, b, preferred_element_type=F32, precision=precision)
    return lax.dot_general(a, b, dims, preferred_element_type=F32, precision=precision)


def _silu(x):
    return x * jax.nn.sigmoid(x)


def _params(sem):
    return pltpu.CompilerParams(dimension_semantics=sem, vmem_limit_bytes=VMEM_LIMIT)


def _in_proj_body(x_ref, xp_ref, xn_ref, nw_ref, w_ref, qkw_ref, cos_ref, sin_ref, cw_ref, alog_ref, dtb_ref,
                  aq_ref, ak_ref, av_ref, qkv_ref, gate_ref, mq_ref, z_ref, ext_ref, *, tm, n_seq_tiles):
    def pre_norm(x):
        ms = jnp.mean(x * x, axis=-1, keepdims=True)
        return (x * lax.rsqrt(ms + EPS) * nw_ref[...]).astype(BF16)

    h = pre_norm(x_ref[...])

    cos = cos_ref[...]
    sin = sin_ref[...]
    lane = lax.broadcasted_iota(jnp.int32, cos.shape, 1)
    first_half = (lane % (ROPE_AXIS_DIM)) < (ROPE_AXIS_DIM // 2)

    def norm_rope(y, w):
        yn = y * lax.rsqrt(jnp.mean(y * y, axis=-1, keepdims=True) + EPS) * w
        swapped = jnp.where(first_half,
                            pltpu.roll(yn, HEAD_DIM - ROPE_AXIS_DIM // 2, 1),
                            pltpu.roll(yn, ROPE_AXIS_DIM // 2, 1))
        return yn * cos + swapped * sin

    col_v = QK_WIDTH
    col_b = col_v + A_KV_WIDTH
    col_g = col_b + B_QKV_WIDTH
    col_m = col_g + GATE_WIDTH
    col_z = col_m + M_WIDTH

    ti = pl.program_id(0) % n_seq_tiles
    halo = BF16_ROWS
    h_ext = jnp.concatenate([pre_norm(xp_ref[...]), h, pre_norm(xn_ref[...])], axis=0)
    bw = 2 * HEAD_DIM

    def b_proj(c0):
        r = _dot(h_ext, w_ref[:, col_b + c0:col_b + c0 + bw])
        ext_ref[0:halo, c0:c0 + bw] = jnp.where(ti == 0, 0.0, r[0:halo])
        ext_ref[halo:halo + tm, c0:c0 + bw] = r[halo:halo + tm]
        ext_ref[halo + tm:, c0:c0 + bw] = jnp.where(ti == n_seq_tiles - 1, 0.0, r[halo + tm:])

    def conv_head(hh):
        cs = slice(hh * HEAD_DIM, (hh + 1) * HEAD_DIM)
        y = None
        for j in range(CONV_K):
            term = ext_ref[pl.ds(halo - CONV_K // 2 + j, tm), cs] * cw_ref[j:j + 1, cs]
            y = term if y is None else y + term
        y = _silu(y)
        if hh < 2 * B_HEADS:
            y = y * lax.rsqrt(jnp.sum(y * y, axis=-1, keepdims=True) + EPS)
        if hh < B_HEADS:
            y = y * (HEAD_DIM ** -0.5)
        qkv_ref[:, cs] = y.astype(BF16)

    def qk_pair(c0):
        y = _dot(h, w_ref[:, c0:c0 + 2 * HEAD_DIM])
        for j in range(2):
            hc = c0 + j * HEAD_DIM
            yh = y[:, j * HEAD_DIM:(j + 1) * HEAD_DIM]
            if hc < A_WIDTH:
                out = norm_rope(yh, qkw_ref[0:1, :]) * (HEAD_DIM ** -0.5 * LOG2E)
                aq_ref[hc:hc + HEAD_DIM, :] = out.T.astype(BF16)
            else:
                out = norm_rope(yh, qkw_ref[1:2, :])
                ak_ref[:, hc - A_WIDTH:hc - A_WIDTH + HEAD_DIM] = out.astype(BF16)

    def v_block():
        vt = _dot(h, w_ref[:, col_v:col_v + A_KV_WIDTH]).T.astype(BF16)
        pad_row = lax.broadcasted_iota(jnp.int32, (V_ROWS - HEAD_DIM, vt.shape[1]), 0)
        ones_pad = jnp.where(pad_row == 0, 1.0, 0.0).astype(BF16)
        for kv in range(A_KV_HEADS):
            av_ref[kv * V_ROWS:kv * V_ROWS + HEAD_DIM, :] = vt[kv * HEAD_DIM:(kv + 1) * HEAD_DIM]
            av_ref[kv * V_ROWS + HEAD_DIM:(kv + 1) * V_ROWS, :] = ones_pad

    def gate_block():
        g = _dot(h, w_ref[:, col_g:col_g + GATE_WIDTH])
        glane = lax.broadcasted_iota(jnp.int32, g.shape, 1) % HEAD_DIM
        t = g + dtb_ref[...]
        softplus = jnp.maximum(t, 0.0) + jnp.log1p(jnp.exp(-jnp.abs(t)))
        glog = -jnp.exp(alog_ref[...]) * softplus
        beta = jax.nn.sigmoid(g)
        gate_ref[...] = jnp.where(glane < B_HEADS, glog, jnp.where(glane < 2 * B_HEADS, beta, 0.0))

    def mq_block():
        mq_ref[...] = (_dot(h, w_ref[:, col_m:col_m + M_WIDTH]) * (HEAD_DIM ** -0.5)).astype(BF16)

    def z_block(c0):
        z_ref[:, c0:c0 + 512] = _dot(h, w_ref[:, col_z + c0:col_z + c0 + 512]).astype(BF16)

    for c0 in range(0, QK_WIDTH, 2 * HEAD_DIM):
        qk_pair(c0)
    v_block()
    for c0 in range(0, B_QKV_WIDTH, bw):
        b_proj(c0)
    for hh in range(B_QKV_WIDTH // HEAD_DIM):
        conv_head(hh)
    gate_block()
    mq_block()
    for c0 in range(0, D_MIX, 512):
        z_block(c0)


def _in_proj(x2, nw, w_all, qkw, cos_t, sin_t, cw, alog_v, dtb_v, seq, tm):
    rows = x2.shape[0]
    n_seq_tiles = seq // tm
    wtot = w_all.shape[1]
    row = lambda i: (i, 0)
    const = lambda i: (0, 0)
    nb = rows // seq
    halo = BF16_ROWS
    per = tm // halo
    last_blk = rows // halo - 1
    outs = [
        (A_KV_WIDTH, BF16), (B_QKV_WIDTH, BF16), (GATE_WIDTH, F32), (M_WIDTH, BF16), (D_MIX, BF16),
    ]
    t_specs = [
        pl.BlockSpec((None, None, A_WIDTH, tm), lambda i: (i // n_seq_tiles, i % n_seq_tiles, 0, 0)),
        pl.BlockSpec((None, None, A_KV_HEADS * V_ROWS, tm), lambda i: (i // n_seq_tiles, i % n_seq_tiles, 0, 0)),
    ]
    t_shapes = [
        jax.ShapeDtypeStruct((nb, n_seq_tiles, A_WIDTH, tm), BF16),
        jax.ShapeDtypeStruct((nb, n_seq_tiles, A_KV_HEADS * V_ROWS, tm), BF16),
    ]
    return pl.pallas_call(
        functools.partial(_in_proj_body, tm=tm, n_seq_tiles=n_seq_tiles),
        grid=(rows // tm,),
        in_specs=[
            pl.BlockSpec((tm, D_MODEL), row),
            pl.BlockSpec((halo, D_MODEL), lambda i: (jnp.maximum(i * per - 1, 0), 0)),
            pl.BlockSpec((halo, D_MODEL), lambda i: (jnp.minimum((i + 1) * per, last_blk), 0)),
            pl.BlockSpec((1, D_MODEL), const),
            pl.BlockSpec((D_MODEL, wtot), const, pipeline_mode=pl.Buffered(1)),
            pl.BlockSpec((SUBLANES, HEAD_DIM), const),
            pl.BlockSpec((tm, HEAD_DIM), lambda i: (i % n_seq_tiles, 0)),
            pl.BlockSpec((tm, HEAD_DIM), lambda i: (i % n_seq_tiles, 0)),
            pl.BlockSpec((SUBLANES, B_QKV_WIDTH), const),
            pl.BlockSpec((1, GATE_WIDTH), const),
            pl.BlockSpec((1, GATE_WIDTH), const),
        ],
        out_specs=[t_specs[0], pl.BlockSpec((tm, outs[0][0]), row), t_specs[1]]
        + [pl.BlockSpec((tm, w), row) for w, _ in outs[1:]],
        out_shape=[t_shapes[0], jax.ShapeDtypeStruct((rows, outs[0][0]), outs[0][1]), t_shapes[1]]
        + [jax.ShapeDtypeStruct((rows, w), dt) for w, dt in outs[1:]],
        scratch_shapes=[pltpu.VMEM((tm + 2 * halo, B_QKV_WIDTH), F32)],
        compiler_params=_params(("parallel",)),
        name="in_proj",
    )(x2, x2, x2, nw, w_all, qkw, cos_t, sin_t, cw, alog_v, dtb_v)


def _attn_body(qt_ref, k_ref, vt_ref, z_ref, o_ref, m_ref, acc_ref, s_ref, p_ref, *, n_kv, n_q, tq):
    def scores(qi, j, slot):
        kt = k_ref[j]
        for g in range(A_GROUP):
            s_ref[slot, g] = _dot(kt, qt_ref[qi, g * HEAD_DIM:(g + 1) * HEAD_DIM, :])

    tk = s_ref.shape[2]
    sub = SUBLANES
    rc = 4 * sub

    def softmax_pv(j, slot):
        vt = vt_ref[j]
        for g in range(A_GROUP):
            mx = s_ref[slot, g, 0:sub, :]
            for c in range(1, tk // sub):
                mx = jnp.maximum(mx, s_ref[slot, g, c * sub:(c + 1) * sub, :])
            m_new = jnp.max(mx, axis=0, keepdims=True)
            if j > 0:
                m_old = m_ref[g]
                m_new = jnp.maximum(m_old, m_new)
                alpha = jnp.exp2(m_old - m_new)
            m_ref[g] = m_new
            for c in range(tk // rc):
                pc = jnp.exp2(s_ref[slot, g, c * rc:(c + 1) * rc, :] - m_new)
                p_ref[slot, g, c * rc:(c + 1) * rc, :] = pc.astype(BF16)
            pv = _dot(vt, p_ref[slot, g])
            if j == 0:
                acc_ref[g] = pv
            else:
                acc_ref[g] = alpha * acc_ref[g] + pv

    scores(0, 0, 0)

    def q_tile(qi, carry):
        for j in range(n_kv):
            slot = j % 2
            if j + 1 < n_kv:
                scores(qi, j + 1, 1 - slot)
            else:
                scores(jnp.minimum(qi + 1, n_q - 1), 0, 1 - slot)
            softmax_pv(j, slot)
        rows = pl.ds(pl.multiple_of(qi * tq, tq), tq)
        for g in range(A_GROUP):
            hs = slice(g * HEAD_DIM, (g + 1) * HEAD_DIM)
            y = (acc_ref[g, 0:HEAD_DIM, :] / acc_ref[g, HEAD_DIM:HEAD_DIM + 1, :]).T
            o_ref[rows, hs] = (y * _silu(z_ref[rows, hs].astype(F32))).astype(BF16)
        return carry

    lax.fori_loop(0, n_q, q_tile, 0)


def _attention(aqt, ak, avt, z):
    b, n_q, _, tq = aqt.shape
    n_kv, tk = avt.shape[1], avt.shape[3]
    seq = n_kv * tk
    assert n_kv % 2 == 0
    gw = A_GROUP * HEAD_DIM
    akr = ak.reshape(b, n_kv, tk, A_KV_WIDTH)
    return pl.pallas_call(
        functools.partial(_attn_body, n_kv=n_kv, n_q=n_q, tq=tq),
        grid=(b, A_KV_HEADS),
        in_specs=[
            pl.BlockSpec((None, n_q, gw, tq), lambda bi, kv: (bi, 0, kv, 0)),
            pl.BlockSpec((None, n_kv, tk, HEAD_DIM), lambda bi, kv: (bi, 0, 0, kv)),
            pl.BlockSpec((None, n_kv, V_ROWS, tk), lambda bi, kv: (bi, 0, kv, 0)),
            pl.BlockSpec((None, seq, gw), lambda bi, kv: (bi, 0, kv)),
        ],
        out_specs=pl.BlockSpec((None, seq, gw), lambda bi, kv: (bi, 0, kv)),
        out_shape=jax.ShapeDtypeStruct((b, seq, A_WIDTH), BF16),
        scratch_shapes=[pltpu.VMEM((A_GROUP, 1, tq), F32), pltpu.VMEM((A_GROUP, V_ROWS, tq), F32), pltpu.VMEM((2, A_GROUP, tk, tq), F32),
                        pltpu.VMEM((2, A_GROUP, tk, tq), BF16)],
        compiler_params=_params(("parallel", "arbitrary")),
        name="attn",
    )(aqt, akr, avt, z)


def _split3(a):
    hi = a.astype(BF16)
    r1 = a - hi.astype(F32)
    mid = r1.astype(BF16)
    lo = (r1 - mid.astype(F32)).astype(BF16)
    return [hi, mid, lo]


def _lane_blocks(cols, width, block):
    rows = cols[0].shape[0]
    blk = lax.broadcasted_iota(jnp.int32, (rows, width), 1) // block
    out = jnp.broadcast_to(cols[-1], (rows, width))
    for i in range(len(cols) - 2, -1, -1):
        out = jnp.where(blk == i, cols[i], out)
    return out


def _tile_rows(x, n):
    return jnp.concatenate([x] * n, axis=0)


def _gdn_body(qf_ref, kf_ref, vf_ref, gf_ref, qb_ref, kb_ref, vb_ref, gb_ref, of_ref, ob_ref,
              s_ref, mq_ref, bo_ref, dc_ref, *, gsz):
    c_, h_, w_ = CHUNK, B_HEADS, B_HEADS * CHUNK

    @pl.when(pl.program_id(1) == 0)
    def _():
        s_ref[...] = jnp.zeros_like(s_ref)
        mq_ref[...] = jnp.zeros_like(mq_ref)
        bo_ref[...] = jnp.zeros_like(bo_ref)
        dc_ref[...] = jnp.zeros_like(dc_ref)

    o_refs = (of_ref, ob_ref)
    for g in range(gsz):
        for d in range(N_DIR):
            rows = g if d == 0 else gsz - 1 - g
            for h in range(h_):
                state = s_ref[d * h_ + h]
                r = _dot(mq_ref[g, d, h], state.astype(BF16))
                bo = bo_ref[g, d, h]
                s_ref[d * h_ + h] = state * dc_ref[g, d, h:h + 1, :] + r[:HEAD_DIM] + bo[:HEAD_DIM]
                o_refs[d][rows * c_:(rows + 1) * c_, h * HEAD_DIM:(h + 1) * HEAD_DIM] = (
                    r[HEAD_DIM:] + bo[HEAD_DIM:]).astype(BF16)

    ri = lax.broadcasted_iota(jnp.int32, (c_, w_), 0)
    jj = lax.broadcasted_iota(jnp.int32, (c_, w_), 1) % c_
    eye_w = (ri == jj).astype(F32)
    r2 = lax.broadcasted_iota(jnp.int32, (c_, c_), 0)
    c2 = lax.broadcasted_iota(jnp.int32, (c_, c_), 1)

    def blockdiag_mask(width, lanes_per_head, period=None):
        rh = lax.broadcasted_iota(jnp.int32, (w_, width), 0) // c_
        ln = lax.broadcasted_iota(jnp.int32, (w_, width), 1)
        if period is not None:
            ln = ln % period
        return (rh == ln // lanes_per_head).astype(BF16)

    bd_mask = blockdiag_mask(w_, c_)
    bdk_mask = blockdiag_mask(B_WIDTH, HEAD_DIM)
    bdr_mask = blockdiag_mask(2 * B_WIDTH, HEAD_DIM, period=B_WIDTH)

    dirs = []
    for d in range(N_DIR):
        diff = (ri - jj) if d == 0 else (jj - ri)
        d2 = (r2 - c2) if d == 0 else (c2 - r2)
        dirs.append(dict(incl=diff >= 0, strict=diff > 0, m2=(diff > 0).astype(F32),
                         m1=(d2 >= 0).astype(BF16)))
    row8 = lax.broadcasted_iota(jnp.int32, (SUBLANES, HEAD_DIM), 0)
    srcs = ((qf_ref, kf_ref, vf_ref, gf_ref), (qb_ref, kb_ref, vb_ref, gb_ref))

    inst = []
    for g in range(gsz):
        for d in range(N_DIR):
            q_ref, k_ref, v_ref, g_ref = srcs[d]
            pos = g if d == 0 else gsz - 1 - g
            sl = slice(pos * c_, (pos + 1) * c_)
            kc = k_ref[sl, :]
            kq = _dot(jnp.concatenate([kc, q_ref[sl, :]], axis=0), _tile_rows(kc, h_) * bdk_mask, _NT)
            inst.append(dict(g=g, d=d, m=dirs[d], sl=sl, q_ref=q_ref, k_ref=k_ref, v_ref=v_ref, g_ref=g_ref,
                             kk_w=kq[:c_], qk_w=kq[c_:]))
    for t in inst:
        m = t["m"]
        gt = t["g_ref"][t["sl"], :]
        gl_w = _lane_blocks([gt[:, h:h + 1] for h in range(h_)], w_, c_)
        d3 = _dot(m["m1"], jnp.concatenate(_split3(gl_w * m["m2"]) + _split3(gt), axis=1))
        dm_w = d3[:, :w_] + d3[:, w_:2 * w_] + d3[:, 2 * w_:3 * w_]
        o3 = 3 * w_
        t["gcum"] = d3[:, o3:o3 + HEAD_DIM] + d3[:, o3 + HEAD_DIM:o3 + 2 * HEAD_DIM] + d3[:, o3 + 2 * HEAD_DIM:]
        t["gt"] = gt
        t["decay"] = jnp.exp(dm_w)
        beta_w = _lane_blocks([gt[:, h_ + h:h_ + h + 1] for h in range(h_)], w_, c_)
        t["x"] = -jnp.where(m["strict"], t["kk_w"] * beta_w * t["decay"], 0.0)

    for t in inst:
        xb = t["x"].astype(BF16)
        t["xp"] = _dot(xb, _tile_rows(xb, h_) * bd_mask)
        t["p"] = eye_w + t["x"]
    for _ in range(4):
        for t in inst:
            xpb = t["xp"].astype(BF16)
            r = _dot(jnp.concatenate([t["p"].astype(BF16), xpb], axis=0), _tile_rows(xpb, h_) * bd_mask)
            t["p"] = t["p"] + r[:c_]
            t["xp"] = r[c_:]
    for t in inst:
        t["p"] = t["p"] + _dot(t["p"].astype(BF16), _tile_rows(t["xp"].astype(BF16), h_) * bd_mask)

    for t in inst:
        gt, sl = t["gt"], t["sl"]
        kf = t["k_ref"][sl, :].astype(F32)
        vf = t["v_ref"][sl, :].astype(F32)
        g_hd = _lane_blocks([t["gcum"][:, h:h + 1] for h in range(h_)], B_WIDTH, HEAD_DIM)
        beta_hd = _lane_blocks([gt[:, h_ + h:h_ + h + 1] for h in range(h_)], B_WIDTH, HEAD_DIM)
        gtot = jnp.sum(gt, axis=0, keepdims=True)
        gtot_hd = _lane_blocks([gtot[:, h:h + 1] for h in range(h_)], B_WIDTH, HEAD_DIM)
        t["eg_hd"] = jnp.exp(g_hd)
        t["dc_hd"] = jnp.exp(gtot_hd)
        t["k_tail"] = (kf * jnp.exp(gtot_hd - g_hd)).astype(BF16)
        rhs = jnp.concatenate([vf * beta_hd, kf * (beta_hd * t["eg_hd"])], axis=1).astype(BF16)
        t["uw"] = _dot(t["p"].astype(BF16), _tile_rows(rhs, h_) * bdr_mask)
    for t in inst:
        intra = jnp.where(t["m"]["incl"], t["qk_w"] * t["decay"], 0.0).astype(BF16)
        t["iuw"] = _dot(intra, _tile_rows(t["uw"].astype(BF16), h_) * bdr_mask)
    for t in inst:
        uw = t["uw"]
        t["mb"] = []
        for h in range(h_):
            hs = slice(h * HEAD_DIM, (h + 1) * HEAD_DIM)
            wu = jnp.concatenate([-uw[:, B_WIDTH + h * HEAD_DIM:B_WIDTH + (h + 1) * HEAD_DIM], uw[:, hs]],
                                 axis=1).astype(BF16)
            t["mb"].append(_dot(t["k_tail"][:, hs], wu, _TN))
    for t in inst:
        g, d = t["g"], t["d"]
        qf = t["q_ref"][t["sl"], :].astype(F32)
        o_local = t["iuw"][:, :B_WIDTH]
        q_eff = qf * t["eg_hd"] - t["iuw"][:, B_WIDTH:]
        dc_tile = jnp.zeros((SUBLANES, HEAD_DIM), F32)
        for h in range(h_):
            hs = slice(h * HEAD_DIM, (h + 1) * HEAD_DIM)
            mb = t["mb"][h]
            mq_ref[g, d, h, 0:HEAD_DIM, :] = mb[:, :HEAD_DIM].astype(BF16)
            mq_ref[g, d, h, HEAD_DIM:, :] = q_eff[:, hs].astype(BF16)
            bo_ref[g, d, h, 0:HEAD_DIM, :] = mb[:, HEAD_DIM:]
            bo_ref[g, d, h, HEAD_DIM:, :] = o_local[:, hs]
            dc_tile = jnp.where(row8 == h, t["dc_hd"][:, hs], dc_tile)
        dc_ref[g, d] = dc_tile


def _gdn(qkvn, gates, gsz):
    b, seq, _ = qkvn.shape
    nb = seq // (gsz * CHUNK)
    rows = gsz * CHUNK
    mrows = HEAD_DIM + CHUNK

    def fwd(i):
        return jnp.minimum(i, nb - 1)

    def bwd(i):
        return nb - 1 - jnp.minimum(i, nb - 1)

    def prev(i):
        return jnp.maximum(i - 1, 0)

    out = jax.ShapeDtypeStruct((b, seq, B_WIDTH), BF16)
    return pl.pallas_call(
        functools.partial(_gdn_body, gsz=gsz),
        grid=(b, nb + 1),
        in_specs=[pl.BlockSpec((None, rows, B_WIDTH), lambda bi, i, j=j: (bi, fwd(i), j)) for j in range(3)]
        + [pl.BlockSpec((None, rows, HEAD_DIM), lambda bi, i: (bi, fwd(i), 0))]
        + [pl.BlockSpec((None, rows, B_WIDTH), lambda bi, i, j=j: (bi, bwd(i), j)) for j in range(3)]
        + [pl.BlockSpec((None, rows, HEAD_DIM), lambda bi, i: (bi, bwd(i), 1))],
        out_specs=[
            pl.BlockSpec((None, rows, B_WIDTH), lambda bi, i: (bi, prev(i), 0)),
            pl.BlockSpec((None, rows, B_WIDTH), lambda bi, i: (bi, nb - 1 - prev(i), 0)),
        ],
        out_shape=[out, out],
        scratch_shapes=[
            pltpu.VMEM((N_DIR * B_HEADS, HEAD_DIM, HEAD_DIM), F32),
            pltpu.VMEM((gsz, N_DIR, B_HEADS, mrows, HEAD_DIM), BF16),
            pltpu.VMEM((gsz, N_DIR, B_HEADS, mrows, HEAD_DIM), F32),
            pltpu.VMEM((gsz, N_DIR, SUBLANES, HEAD_DIM), F32),
        ],
        compiler_params=_params(("parallel", "arbitrary")),
        name="gdn",
    )(qkvn, qkvn, qkvn, gates, qkvn, qkvn, qkvn, gates)


def _mem_kv_body(m_ref, nw_ref, w_ref, o_ref):
    x = m_ref[...]
    ms = jnp.mean(x * x, axis=-1, keepdims=True)
    h = (x * lax.rsqrt(ms + EPS) * nw_ref[...]).astype(BF16)
    o_ref[...] = _dot(h, w_ref[...]).astype(BF16)


def _mem_kv(mem2, nw, w, tm):
    rows = mem2.shape[0]
    return pl.pallas_call(
        _mem_kv_body,
        grid=(rows // tm,),
        in_specs=[
            pl.BlockSpec((tm, D_MODEL), lambda i: (i, 0)),
            pl.BlockSpec((1, D_MODEL), lambda i: (0, 0)),
            pl.BlockSpec((D_MODEL, 2 * M_WIDTH), lambda i: (0, 0)),
        ],
        out_specs=pl.BlockSpec((tm, 2 * M_WIDTH), lambda i: (i, 0)),
        out_shape=jax.ShapeDtypeStruct((rows, 2 * M_WIDTH), BF16),
        compiler_params=_params(("parallel",)),
        name="mem_kv",
    )(mem2, nw, w)


def _mem_attn_body(q_ref, kv_ref, z_ref, o_ref):
    for h in range(M_HEADS):
        hs = slice(h * HEAD_DIM, (h + 1) * HEAD_DIM)
        s = _dot(q_ref[:, hs], kv_ref[:, hs], _NT)
        m = jnp.max(s, axis=-1, keepdims=True)
        p = jnp.exp(s - m)
        l = jnp.sum(p, axis=-1, keepdims=True)
        vh = kv_ref[:, M_WIDTH + h * HEAD_DIM:M_WIDTH + (h + 1) * HEAD_DIM]
        y = _dot(p.astype(BF16), vh) / l
        o_ref[:, hs] = (y * _silu(z_ref[:, hs].astype(F32))).astype(BF16)


def _mem_attn(mq, mkv, z, tq):
    b, seq, _ = mq.shape
    n_mem = mkv.shape[1]
    z_blk = (A_WIDTH + B_WIDTH) // M_WIDTH
    return pl.pallas_call(
        _mem_attn_body,
        grid=(b, seq // tq),
        in_specs=[
            pl.BlockSpec((None, tq, M_WIDTH), lambda bi, i: (bi, i, 0)),
            pl.BlockSpec((None, n_mem, 2 * M_WIDTH), lambda bi, i: (bi, 0, 0)),
            pl.BlockSpec((None, tq, M_WIDTH), lambda bi, i: (bi, i, z_blk)),
        ],
        out_specs=pl.BlockSpec((None, tq, M_WIDTH), lambda bi, i: (bi, i, 0)),
        out_shape=jax.ShapeDtypeStruct((b, seq, M_WIDTH), BF16),
        compiler_params=_params(("parallel", "parallel")),
        name="mem_attn",
    )(mq, mkv, z)


def _out_proj_body(ya_ref, of_ref, ob_ref, zb_ref, ym_ref, dnw_ref, w_ref, pw_ref, x_ref, o_ref):
    ob = of_ref[...].astype(F32) + ob_ref[...].astype(F32)
    dnw = dnw_ref[...]
    parts = []
    for h in range(B_HEADS):
        hs = slice(h * HEAD_DIM, (h + 1) * HEAD_DIM)
        oh = ob[:, hs]
        yn = oh * lax.rsqrt(jnp.mean(oh * oh, axis=-1, keepdims=True) + EPS) * dnw
        parts.append((yn * _silu(zb_ref[:, hs].astype(F32))).astype(BF16))
    yb = jnp.concatenate(parts, axis=1)
    acc = _dot(ya_ref[...], w_ref[0:A_WIDTH, :])
    acc = acc + _dot(yb, w_ref[A_WIDTH:A_WIDTH + B_WIDTH, :])
    acc = acc + _dot(ym_ref[...], w_ref[A_WIDTH + B_WIDTH:, :])
    y = acc * lax.rsqrt(jnp.mean(acc * acc, axis=-1, keepdims=True) + EPS) * pw_ref[...]
    o_ref[...] = x_ref[...] + y


def _out_proj(ya, o_f, o_b, z, ym, dnw, w, pw, x, tm):
    b, seq, _ = ya.shape
    z_blk = A_WIDTH // B_WIDTH
    row = lambda bi, i: (bi, i, 0)
    const = lambda bi, i: (0, 0)
    return pl.pallas_call(
        _out_proj_body,
        grid=(b, seq // tm),
        in_specs=[
            pl.BlockSpec((None, tm, A_WIDTH), row),
            pl.BlockSpec((None, tm, B_WIDTH), row),
            pl.BlockSpec((None, tm, B_WIDTH), row),
            pl.BlockSpec((None, tm, B_WIDTH), lambda bi, i: (bi, i, z_blk)),
            pl.BlockSpec((None, tm, M_WIDTH), row),
            pl.BlockSpec((1, HEAD_DIM), const),
            pl.BlockSpec((D_MIX, D_MODEL), const),
            pl.BlockSpec((1, D_MODEL), const),
            pl.BlockSpec((None, tm, D_MODEL), row),
        ],
        out_specs=pl.BlockSpec((None, tm, D_MODEL), row),
        out_shape=jax.ShapeDtypeStruct((b, seq, D_MODEL), F32),
        compiler_params=_params(("parallel", "parallel")),
        name="out_proj",
    )(ya, o_f, o_b, z, ym, dnw, w, pw, x)


def _rope_tables(seq):
    t = np.arange(seq)
    inv_freq = (np.float32(ROPE_THETA) ** (-np.arange(0, ROPE_AXIS_DIM, 2, dtype=np.float32) / ROPE_AXIS_DIM)
                ).astype(np.float32)
    ang_row = (t // GRID_W).astype(np.float32)[:, None] * inv_freq
    ang_col = (t % GRID_W).astype(np.float32)[:, None] * inv_freq
    cr, sr, cc, sc = np.cos(ang_row), np.sin(ang_row), np.cos(ang_col), np.sin(ang_col)
    return (jnp.asarray(np.concatenate([cr, cr, cc, cc], axis=-1), F32),
            jnp.asarray(np.concatenate([-sr, sr, -sc, sc], axis=-1), F32))


def _arrange_w_in(w):
    pts = np.cumsum([A_WIDTH, A_KV_WIDTH, A_KV_WIDTH, B_QKV_WIDTH, N_DIR * B_HEADS, N_DIR * B_HEADS, M_WIDTH])
    aq, ak, av, bqkv, ba, bb, mq, z = jnp.split(w, pts, axis=1)
    gate_blocks = []
    for d in range(N_DIR):
        sl = slice(d * B_HEADS, (d + 1) * B_HEADS)
        pad = jnp.zeros((w.shape[0], HEAD_DIM - 2 * B_HEADS), w.dtype)
        gate_blocks += [ba[:, sl], bb[:, sl], pad]
    return jnp.concatenate([aq, ak, av, bqkv] + gate_blocks + [mq, z], axis=1).astype(BF16)


def _gate_lane_vec(p):
    out = jnp.zeros((N_DIR, HEAD_DIM), F32).at[:, :B_HEADS].set(p.astype(F32))
    return out.reshape(1, GATE_WIDTH)


def kernel(x, mem, norm_pre_w, w_in, q_norm_w, k_norm_w, conv_w, a_log, dt_bias, delta_norm_w,
           mem_norm_w, w_mem_kv, w_out, norm_post_w):
    b, seq, _ = x.shape
    n_mem = mem.shape[1]
    assert seq % 512 == 0 and seq % GRID_W == 0 and w_in.shape[0] == 1
    l = 0
    tm = 512

    cos_t, sin_t = _rope_tables(seq)
    qkw = jnp.zeros((SUBLANES, HEAD_DIM), F32).at[0].set(q_norm_w[l]).at[1].set(k_norm_w[l])
    cw = jnp.zeros((SUBLANES, B_QKV_WIDTH), F32).at[:CONV_K].set(conv_w[l])
    aq, ak, av, qkvn, gates, mq, z = _in_proj(
        x.reshape(b * seq, D_MODEL), norm_pre_w[l].reshape(1, D_MODEL), _arrange_w_in(w_in[l]),
        qkw, cos_t, sin_t, cw, _gate_lane_vec(a_log[l]), _gate_lane_vec(dt_bias[l]), seq, tm)
    shp = lambda a: a.reshape(b, seq, a.shape[-1])
    ak, qkvn, gates, mq, z = map(shp, (ak, qkvn, gates, mq, z))

    ya = _attention(aq, ak, av, z)
    o_f, o_b = _gdn(qkvn, gates, gsz=4)

    mkv = _mem_kv(mem.reshape(b * n_mem, D_MODEL), mem_norm_w[l].reshape(1, D_MODEL),
                  w_mem_kv[l].astype(BF16), tm=256)
    ym = _mem_attn(mq, mkv.reshape(b, n_mem, 2 * M_WIDTH), z, tq=512)

    return _out_proj(ya, o_f, o_b, z, ym, delta_norm_w[l].reshape(1, HEAD_DIM), w_out[l].astype(BF16),
                     norm_post_w[l].reshape(1, D_MODEL), x, tm=512)
```

```python
import functools

import jax
import jax.numpy as jnp
import numpy as np
from jax import lax
from jax.experimental import pallas as pl
from jax.experimental.pallas import tpu as pltpu

F32 = jnp.float32
BF16 = jnp.bfloat16

D_MODEL = 1024
HEAD_DIM = 128
GRID_W = 64
A_HEADS = 8
A_KV_HEADS = 2
A_GROUP = A_HEADS // A_KV_HEADS
A_WIDTH = A_HEADS * HEAD_DIM
A_KV_WIDTH = A_KV_HEADS * HEAD_DIM
B_HEADS = 4
B_WIDTH = B_HEADS * HEAD_DIM
B_QKV_WIDTH = 3 * B_WIDTH
N_DIR = 2
CONV_K = 5
CHUNK = 64
M_HEADS = 4
M_WIDTH = M_HEADS * HEAD_DIM
D_MIX = A_WIDTH + B_WIDTH + M_WIDTH
ROPE_AXIS_DIM = HEAD_DIM // 2
ROPE_THETA = 10000.0
EPS = 1e-6
LOG2E = 1.4426950408889634
GATE_WIDTH = N_DIR * HEAD_DIM
QK_WIDTH = A_WIDTH + A_KV_WIDTH
SUBLANES = 8
BF16_ROWS = 2 * SUBLANES
V_ROWS = HEAD_DIM + BF16_ROWS
VMEM_LIMIT = 56 * 1024 * 1024

_NT = (((1,), (1,)), ((), ()))
_TN = (((0,), (0,)), ((), ()))


def _dot(a, b, dims=None, precision=None):
    if dims is None:
        return jnp.dot(a, b, preferred_element_type=F32, precision=precision)
    return lax.dot_general(a, b, dims, preferred_element_type=F32, precision=precision)


def _silu(x):
    return x * jax.nn.sigmoid(x)


def _params(sem):
    return pltpu.CompilerParams(dimension_semantics=sem, vmem_limit_bytes=VMEM_LIMIT)


def _in_proj_body(x_ref, xp_ref, xn_ref, nw_ref, w_ref, qkw_ref, cos_ref, sin_ref, cw_ref, alog_ref, dtb_ref,
                  aq_ref, ak_ref, av_ref, qkv_ref, gate_ref, mq_ref, z_ref, ext_ref, *, tm, n_seq_tiles):
    def pre_norm(x):
        ms = jnp.mean(x * x, axis=-1, keepdims=True)
        return (x * lax.rsqrt(ms + EPS) * nw_ref[...]).astype(BF16)

    h = pre_norm(x_ref[...])

    cos = cos_ref[...]
    sin = sin_ref[...]
    lane = lax.broadcasted_iota(jnp.int32, cos.shape, 1)
    first_half = (lane % (ROPE_AXIS_DIM)) < (ROPE_AXIS_DIM // 2)

    def norm_rope(y, w):
        yn = y * lax.rsqrt(jnp.mean(y * y, axis=-1, keepdims=True) + EPS) * w
        swapped = jnp.where(first_half,
                            pltpu.roll(yn, HEAD_DIM - ROPE_AXIS_DIM // 2, 1),
                            pltpu.roll(yn, ROPE_AXIS_DIM // 2, 1))
        return yn * cos + swapped * sin

    col_v = QK_WIDTH
    col_b = col_v + A_KV_WIDTH
    col_g = col_b + B_QKV_WIDTH
    col_m = col_g + GATE_WIDTH
    col_z = col_m + M_WIDTH

    ti = pl.program_id(0) % n_seq_tiles
    halo = BF16_ROWS
    h_ext = jnp.concatenate([pre_norm(xp_ref[...]), h, pre_norm(xn_ref[...])], axis=0)
    bw = 2 * HEAD_DIM

    def b_proj(c0):
        r = _dot(h_ext, w_ref[:, col_b + c0:col_b + c0 + bw])
        ext_ref[0:halo, c0:c0 + bw] = jnp.where(ti == 0, 0.0, r[0:halo])
        ext_ref[halo:halo + tm, c0:c0 + bw] = r[halo:halo + tm]
        ext_ref[halo + tm:, c0:c0 + bw] = jnp.where(ti == n_seq_tiles - 1, 0.0, r[halo + tm:])

    def conv_head(hh):
        cs = slice(hh * HEAD_DIM, (hh + 1) * HEAD_DIM)
        y = None
        for j in range(CONV_K):
            term = ext_ref[pl.ds(halo - CONV_K // 2 + j, tm), cs] * cw_ref[j:j + 1, cs]
            y = term if y is None else y + term
        y = _silu(y)
        if hh < 2 * B_HEADS:
            y = y * lax.rsqrt(jnp.sum(y * y, axis=-1, keepdims=True) + EPS)
        if hh < B_HEADS:
            y = y * (HEAD_DIM ** -0.5)
        qkv_ref[:, cs] = y.astype(BF16)

    def qk_pair(c0):
        y = _dot(h, w_ref[:, c0:c0 + 2 * HEAD_DIM])
        for j in range(2):
            hc = c0 + j * HEAD_DIM
            yh = y[:, j * HEAD_DIM:(j + 1) * HEAD_DIM]
            if hc < A_WIDTH:
                out = norm_rope(yh, qkw_ref[0:1, :]) * (HEAD_DIM ** -0.5 * LOG2E)
                aq_ref[hc:hc + HEAD_DIM, :] = out.T.astype(BF16)
            else:
                out = norm_rope(yh, qkw_ref[1:2, :])
                ak_ref[:, hc - A_WIDTH:hc - A_WIDTH + HEAD_DIM] = out.astype(BF16)

    def v_block():
        vt = _dot(h, w_ref[:, col_v:col_v + A_KV_WIDTH]).T.astype(BF16)
        pad_row = lax.broadcasted_iota(jnp.int32, (V_ROWS - HEAD_DIM, vt.shape[1]), 0)
        ones_pad = jnp.where(pad_row == 0, 1.0, 0.0).astype(BF16)
        for kv in range(A_KV_HEADS):
            av_ref[kv * V_ROWS:kv * V_ROWS + HEAD_DIM, :] = vt[kv * HEAD_DIM:(kv + 1) * HEAD_DIM]
            av_ref[kv * V_ROWS + HEAD_DIM:(kv + 1) * V_ROWS, :] = ones_pad

    def gate_block():
        g = _dot(h, w_ref[:, col_g:col_g + GATE_WIDTH])
        glane = lax.broadcasted_iota(jnp.int32, g.shape, 1) % HEAD_DIM
        t = g + dtb_ref[...]
        softplus = jnp.maximum(t, 0.0) + jnp.log1p(jnp.exp(-jnp.abs(t)))
        glog = -jnp.exp(alog_ref[...]) * softplus
        beta = jax.nn.sigmoid(g)
        gate_ref[...] = jnp.where(glane < B_HEADS, glog, jnp.where(glane < 2 * B_HEADS, beta, 0.0))

    def mq_block():
        mq_ref[...] = (_dot(h, w_ref[:, col_m:col_m + M_WIDTH]) * (HEAD_DIM ** -0.5)).astype(BF16)

    def z_block(c0):
        z_ref[:, c0:c0 + 512] = _dot(h, w_ref[:, col_z + c0:col_z + c0 + 512]).astype(BF16)

    for c0 in range(0, QK_WIDTH, 2 * HEAD_DIM):
        qk_pair(c0)
    v_block()
    for c0 in range(0, B_QKV_WIDTH, bw):
        b_proj(c0)
    for hh in range(B_QKV_WIDTH // HEAD_DIM):
        conv_head(hh)
    gate_block()
    mq_block()
    for c0 in range(0, D_MIX, 512):
        z_block(c0)


def _in_proj(x2, nw, w_all, qkw, cos_t, sin_t, cw, alog_v, dtb_v, seq, tm):
    rows = x2.shape[0]
    n_seq_tiles = seq // tm
    wtot = w_all.shape[1]
    row = lambda i: (i, 0)
    const = lambda i: (0, 0)
    nb = rows // seq
    halo = BF16_ROWS
    per = tm // halo
    last_blk = rows // halo - 1
    outs = [
        (A_KV_WIDTH, BF16), (B_QKV_WIDTH, BF16), (GATE_WIDTH, F32), (M_WIDTH, BF16), (D_MIX, BF16),
    ]
    t_specs = [
        pl.BlockSpec((None, None, A_WIDTH, tm), lambda i: (i // n_seq_tiles, i % n_seq_tiles, 0, 0)),
        pl.BlockSpec((None, None, A_KV_HEADS * V_ROWS, tm), lambda i: (i // n_seq_tiles, i % n_seq_tiles, 0, 0)),
    ]
    t_shapes = [
        jax.ShapeDtypeStruct((nb, n_seq_tiles, A_WIDTH, tm), BF16),
        jax.ShapeDtypeStruct((nb, n_seq_tiles, A_KV_HEADS * V_ROWS, tm), BF16),
    ]
    return pl.pallas_call(
        functools.partial(_in_proj_body, tm=tm, n_seq_tiles=n_seq_tiles),
        grid=(rows // tm,),
        in_specs=[
            pl.BlockSpec((tm, D_MODEL), row),
            pl.BlockSpec((halo, D_MODEL), lambda i: (jnp.maximum(i * per - 1, 0), 0)),
            pl.BlockSpec((halo, D_MODEL), lambda i: (jnp.minimum((i + 1) * per, last_blk), 0)),
            pl.BlockSpec((1, D_MODEL), const),
            pl.BlockSpec((D_MODEL, wtot), const, pipeline_mode=pl.Buffered(1)),
            pl.BlockSpec((SUBLANES, HEAD_DIM), const),
            pl.BlockSpec((tm, HEAD_DIM), lambda i: (i % n_seq_tiles, 0)),
            pl.BlockSpec((tm, HEAD_DIM), lambda i: (i % n_seq_tiles, 0)),
            pl.BlockSpec((SUBLANES, B_QKV_WIDTH), const),
            pl.BlockSpec((1, GATE_WIDTH), const),
            pl.BlockSpec((1, GATE_WIDTH), const),
        ],
        out_specs=[t_specs[0], pl.BlockSpec((tm, outs[0][0]), row), t_specs[1]]
        + [pl.BlockSpec((tm, w), row) for w, _ in outs[1:]],
        out_shape=[t_shapes[0], jax.ShapeDtypeStruct((rows, outs[0][0]), outs[0][1]), t_shapes[1]]
        + [jax.ShapeDtypeStruct((rows, w), dt) for w, dt in outs[1:]],
        scratch_shapes=[pltpu.VMEM((tm + 2 * halo, B_QKV_WIDTH), F32)],
        compiler_params=_params(("parallel",)),
        name="in_proj",
    )(x2, x2, x2, nw, w_all, qkw, cos_t, sin_t, cw, alog_v, dtb_v)


def _attn_body(qt_ref, k_ref, vt_ref, z_ref, o_ref, m_ref, acc_ref, s_ref, p_ref, *, n_kv, n_q, tq):
    def scores(qi, j, slot, g):
        s_ref[slot, g] = _dot(k_ref[j], qt_ref[qi, g * HEAD_DIM:(g + 1) * HEAD_DIM, :])

    tk = s_ref.shape[2]
    sub = SUBLANES
    rc = 4 * sub

    def softmax_pv(j, slot, g):
        mx = s_ref[slot, g, 0:sub, :]
        for c in range(1, tk // sub):
            mx = jnp.maximum(mx, s_ref[slot, g, c * sub:(c + 1) * sub, :])
        m_new = jnp.max(mx, axis=0, keepdims=True)
        if j > 0:
            m_old = m_ref[g]
            m_new = jnp.maximum(m_old, m_new)
            alpha = jnp.exp2(m_old - m_new)
        m_ref[g] = m_new
        for c in range(tk // rc):
            pc = jnp.exp2(s_ref[slot, g, c * rc:(c + 1) * rc, :] - m_new)
            p_ref[slot, g, c * rc:(c + 1) * rc, :] = pc.astype(BF16)
        pv = _dot(vt_ref[j], p_ref[slot, g])
        if j == 0:
            acc_ref[g] = pv
        else:
            acc_ref[g] = alpha * acc_ref[g] + pv

    for g in range(A_GROUP):
        scores(0, 0, 0, g)

    def q_tile(qi, carry):
        for j in range(n_kv):
            slot = j % 2
            for g in range(A_GROUP):
                if j + 1 < n_kv:
                    scores(qi, j + 1, 1 - slot, g)
                else:
                    scores(jnp.minimum(qi + 1, n_q - 1), 0, 1 - slot, g)
                softmax_pv(j, slot, g)
        rows = pl.ds(pl.multiple_of(qi * tq, tq), tq)
        for g in range(A_GROUP):
            hs = slice(g * HEAD_DIM, (g + 1) * HEAD_DIM)
            y = (acc_ref[g, 0:HEAD_DIM, :] / acc_ref[g, HEAD_DIM:HEAD_DIM + 1, :]).T
            o_ref[rows, hs] = (y * _silu(z_ref[rows, hs].astype(F32))).astype(BF16)
        return carry

    lax.fori_loop(0, n_q, q_tile, 0)


def _attention(aqt, ak, avt, z):
    b, n_q, _, tq = aqt.shape
    n_kv, tk = avt.shape[1], avt.shape[3]
    seq = n_kv * tk
    assert n_kv % 2 == 0
    gw = A_GROUP * HEAD_DIM
    akr = ak.reshape(b, n_kv, tk, A_KV_WIDTH)
    return pl.pallas_call(
        functools.partial(_attn_body, n_kv=n_kv, n_q=n_q, tq=tq),
        grid=(b, A_KV_HEADS),
        in_specs=[
            pl.BlockSpec((None, n_q, gw, tq), lambda bi, kv: (bi, 0, kv, 0)),
            pl.BlockSpec((None, n_kv, tk, HEAD_DIM), lambda bi, kv: (bi, 0, 0, kv)),
            pl.BlockSpec((None, n_kv, V_ROWS, tk), lambda bi, kv: (bi, 0, kv, 0)),
            pl.BlockSpec((None, seq, gw), lambda bi, kv: (bi, 0, kv)),
        ],
        out_specs=pl.BlockSpec((None, seq, gw), lambda bi, kv: (bi, 0, kv)),
        out_shape=jax.ShapeDtypeStruct((b, seq, A_WIDTH), BF16),
        scratch_shapes=[pltpu.VMEM((A_GROUP, 1, tq), F32), pltpu.VMEM((A_GROUP, V_ROWS, tq), F32), pltpu.VMEM((2, A_GROUP, tk, tq), F32),
                        pltpu.VMEM((2, A_GROUP, tk, tq), BF16)],
        compiler_params=_params(("parallel", "arbitrary")),
        name="attn",
    )(aqt, akr, avt, z)


def _split2(a):
    hi = a.astype(BF16)
    lo = (a - hi.astype(F32)).astype(BF16)
    return [hi, lo]


def _lane_blocks(cols, width, block):
    rows = cols[0].shape[0]
    blk = lax.broadcasted_iota(jnp.int32, (rows, width), 1) // block
    out = jnp.broadcast_to(cols[-1], (rows, width))
    for i in range(len(cols) - 2, -1, -1):
        out = jnp.where(blk == i, cols[i], out)
    return out


def _tile_rows(x, n):
    return jnp.concatenate([x] * n, axis=0)


def _gdn_body(qf_ref, kf_ref, vf_ref, gf_ref, qb_ref, kb_ref, vb_ref, gb_ref, of_ref, ob_ref,
              s_ref, mq_ref, bo_ref, dc_ref, *, gsz, nb):
    c_, h_, w_ = CHUNK, B_HEADS, B_HEADS * CHUNK
    n_pair = h_ // 2
    pw = 2 * HEAD_DIM
    step = pl.program_id(0)

    @pl.when(step == 0)
    def _():
        s_ref[...] = jnp.zeros_like(s_ref)
        mq_ref[...] = jnp.zeros_like(mq_ref)
        bo_ref[...] = jnp.zeros_like(bo_ref)
        dc_ref[...] = jnp.zeros_like(dc_ref)

    keep = jnp.where((step - 1) % nb == 0, 0.0, 1.0)
    o_refs = (of_ref, ob_ref)
    zero_blk = jnp.zeros((HEAD_DIM, HEAD_DIM), BF16)
    for g in range(gsz):
        for d in range(N_DIR):
            rows = g if d == 0 else gsz - 1 - g
            for pr in range(n_pair):
                state = s_ref[d * n_pair + pr]
                if g == 0:
                    state = state * keep
                sb = state.astype(BF16)
                s_bd = jnp.concatenate([jnp.concatenate([sb[:, :HEAD_DIM], zero_blk], axis=1),
                                        jnp.concatenate([zero_blk, sb[:, HEAD_DIM:]], axis=1)], axis=0)
                r = _dot(mq_ref[g, d, pr], s_bd)
                bo = bo_ref[g, d, pr]
                s_ref[d * n_pair + pr] = (state * dc_ref[g, d, 0:1, pr * pw:(pr + 1) * pw]
                                          + r[:HEAD_DIM] + bo[:HEAD_DIM])
                o_refs[d][rows * c_:(rows + 1) * c_, pr * pw:(pr + 1) * pw] = (
                    r[HEAD_DIM:] + bo[HEAD_DIM:]).astype(BF16)

    ri = lax.broadcasted_iota(jnp.int32, (c_, w_), 0)
    jj = lax.broadcasted_iota(jnp.int32, (c_, w_), 1) % c_
    eye_w = (ri == jj).astype(F32)
    r2 = lax.broadcasted_iota(jnp.int32, (c_, c_), 0)
    c2 = lax.broadcasted_iota(jnp.int32, (c_, c_), 1)

    def blockdiag_mask(width, lanes_per_head, period=None):
        rh = lax.broadcasted_iota(jnp.int32, (w_, width), 0) // c_
        ln = lax.broadcasted_iota(jnp.int32, (w_, width), 1)
        if period is not None:
            ln = ln % period
        return (rh == ln // lanes_per_head).astype(BF16)

    bd_mask = blockdiag_mask(w_, c_)
    bdk_mask = blockdiag_mask(B_WIDTH, HEAD_DIM)
    bdr_mask = blockdiag_mask(2 * B_WIDTH, HEAD_DIM, period=B_WIDTH)

    dirs = []
    for d in range(N_DIR):
        diff = (ri - jj) if d == 0 else (jj - ri)
        d2 = (r2 - c2) if d == 0 else (c2 - r2)
        dirs.append(dict(incl=diff >= 0, strict=diff > 0, m2=(diff > 0).astype(F32),
                         m1=(d2 >= 0).astype(BF16)))
    row8 = lax.broadcasted_iota(jnp.int32, (SUBLANES, HEAD_DIM), 0)
    srcs = ((qf_ref, kf_ref, vf_ref, gf_ref), (qb_ref, kb_ref, vb_ref, gb_ref))

    inst = []
    for g in range(gsz):
        for d in range(N_DIR):
            q_ref, k_ref, v_ref, g_ref = srcs[d]
            pos = g if d == 0 else gsz - 1 - g
            sl = slice(pos * c_, (pos + 1) * c_)
            kc = k_ref[sl, :]
            kq = _dot(jnp.concatenate([kc, q_ref[sl, :]], axis=0), _tile_rows(kc, h_) * bdk_mask, _NT)
            inst.append(dict(g=g, d=d, m=dirs[d], sl=sl, q_ref=q_ref, k_ref=k_ref, v_ref=v_ref, g_ref=g_ref,
                             kk_w=kq[:c_], qk_w=kq[c_:]))
    for t in inst:
        m = t["m"]
        gt = t["g_ref"][t["sl"], :]
        gl_w = _lane_blocks([gt[:, h:h + 1] for h in range(h_)], w_, c_)
        d3 = _dot(m["m1"], jnp.concatenate(_split2(gl_w * m["m2"]) + _split2(gt), axis=1))
        dm_w = d3[:, :w_] + d3[:, w_:2 * w_]
        o3 = 2 * w_
        t["gcum"] = d3[:, o3:o3 + HEAD_DIM] + d3[:, o3 + HEAD_DIM:]
        t["gt"] = gt
        t["decay"] = jnp.exp(dm_w)
        beta_w = _lane_blocks([gt[:, h_ + h:h_ + h + 1] for h in range(h_)], w_, c_)
        t["x"] = -jnp.where(m["strict"], t["kk_w"] * beta_w * t["decay"], 0.0)

    for t in inst:
        xb = t["x"].astype(BF16)
        t["xp"] = _dot(xb, _tile_rows(xb, h_) * bd_mask)
        t["p"] = eye_w + t["x"]
    for _ in range(4):
        for t in inst:
            xpb = t["xp"].astype(BF16)
            r = _dot(jnp.concatenate([t["p"].astype(BF16), xpb], axis=0), _tile_rows(xpb, h_) * bd_mask)
            t["p"] = t["p"] + r[:c_]
            t["xp"] = r[c_:]
    for t in inst:
        t["p"] = t["p"] + _dot(t["p"].astype(BF16), _tile_rows(t["xp"].astype(BF16), h_) * bd_mask)

    for t in inst:
        gt, sl = t["gt"], t["sl"]
        kf = t["k_ref"][sl, :].astype(F32)
        vf = t["v_ref"][sl, :].astype(F32)
        g_hd = _lane_blocks([t["gcum"][:, h:h + 1] for h in range(h_)], B_WIDTH, HEAD_DIM)
        beta_hd = _lane_blocks([gt[:, h_ + h:h_ + h + 1] for h in range(h_)], B_WIDTH, HEAD_DIM)
        gtot = jnp.sum(gt, axis=0, keepdims=True)
        gtot_hd = _lane_blocks([gtot[:, h:h + 1] for h in range(h_)], B_WIDTH, HEAD_DIM)
        t["eg_hd"] = jnp.exp(g_hd)
        t["dc_hd"] = jnp.exp(gtot_hd)
        t["k_tail"] = (kf * jnp.exp(gtot_hd - g_hd)).astype(BF16)
        rhs = jnp.concatenate([vf * beta_hd, kf * (beta_hd * t["eg_hd"])], axis=1).astype(BF16)
        t["uw"] = _dot(t["p"].astype(BF16), _tile_rows(rhs, h_) * bdr_mask)
    for t in inst:
        intra = jnp.where(t["m"]["incl"], t["qk_w"] * t["decay"], 0.0).astype(BF16)
        t["iuw"] = _dot(intra, _tile_rows(t["uw"].astype(BF16), h_) * bdr_mask)
    for t in inst:
        uw = t["uw"]
        t["mb"] = []
        for h in range(h_):
            hs = slice(h * HEAD_DIM, (h + 1) * HEAD_DIM)
            wu = jnp.concatenate([-uw[:, B_WIDTH + h * HEAD_DIM:B_WIDTH + (h + 1) * HEAD_DIM], uw[:, hs]],
                                 axis=1).astype(BF16)
            t["mb"].append(_dot(t["k_tail"][:, hs], wu, _TN))
    for t in inst:
        g, d = t["g"], t["d"]
        qf = t["q_ref"][t["sl"], :].astype(F32)
        o_local = t["iuw"][:, :B_WIDTH]
        q_eff = (qf * t["eg_hd"] - t["iuw"][:, B_WIDTH:]).astype(BF16)
        for h in range(h_):
            pr, ls = h // 2, slice((h % 2) * HEAD_DIM, (h % 2 + 1) * HEAD_DIM)
            mb = t["mb"][h]
            mq_ref[g, d, pr, 0:HEAD_DIM, ls] = mb[:, :HEAD_DIM].astype(BF16)
            bo_ref[g, d, pr, 0:HEAD_DIM, ls] = mb[:, HEAD_DIM:]
        for pr in range(n_pair):
            ps = slice(pr * pw, (pr + 1) * pw)
            mq_ref[g, d, pr, HEAD_DIM:, :] = q_eff[:, ps]
            bo_ref[g, d, pr, HEAD_DIM:, :] = o_local[:, ps]
        dc_ref[g, d] = jnp.broadcast_to(t["dc_hd"], (SUBLANES, B_WIDTH))


def _gdn(qkvn, gates, gsz):
    b, seq, _ = qkvn.shape
    nb = seq // (gsz * CHUNK)
    rows = gsz * CHUNK
    mrows = HEAD_DIM + CHUNK

    total = b * nb
    n_pair = B_HEADS // 2

    def cur(t):
        return jnp.minimum(t, total - 1)

    def prev(t):
        return jnp.maximum(t - 1, 0)

    def fwd_in(j):
        return lambda t: (cur(t) // nb, cur(t) % nb, j)

    def bwd_in(j):
        return lambda t: (cur(t) // nb, nb - 1 - cur(t) % nb, j)

    out = jax.ShapeDtypeStruct((b, seq, B_WIDTH), BF16)
    return pl.pallas_call(
        functools.partial(_gdn_body, gsz=gsz, nb=nb),
        grid=(total + 1,),
        in_specs=[pl.BlockSpec((None, rows, B_WIDTH), fwd_in(j)) for j in range(3)]
        + [pl.BlockSpec((None, rows, HEAD_DIM), fwd_in(0))]
        + [pl.BlockSpec((None, rows, B_WIDTH), bwd_in(j)) for j in range(3)]
        + [pl.BlockSpec((None, rows, HEAD_DIM), bwd_in(1))],
        out_specs=[
            pl.BlockSpec((None, rows, B_WIDTH), lambda t: (prev(t) // nb, prev(t) % nb, 0)),
            pl.BlockSpec((None, rows, B_WIDTH), lambda t: (prev(t) // nb, nb - 1 - prev(t) % nb, 0)),
        ],
        out_shape=[out, out],
        scratch_shapes=[
            pltpu.VMEM((N_DIR * n_pair, HEAD_DIM, 2 * HEAD_DIM), F32),
            pltpu.VMEM((gsz, N_DIR, n_pair, mrows, 2 * HEAD_DIM), BF16),
            pltpu.VMEM((gsz, N_DIR, n_pair, mrows, 2 * HEAD_DIM), F32),
            pltpu.VMEM((gsz, N_DIR, SUBLANES, B_WIDTH), F32),
        ],
        compiler_params=_params(("arbitrary",)),
        name="gdn",
    )(qkvn, qkvn, qkvn, gates, qkvn, qkvn, qkvn, gates)


def _mem_kv_body(m_ref, nw_ref, w_ref, o_ref):
    x = m_ref[...]
    ms = jnp.mean(x * x, axis=-1, keepdims=True)
    h = (x * lax.rsqrt(ms + EPS) * nw_ref[...]).astype(BF16)
    o_ref[...] = _dot(h, w_ref[...]).astype(BF16)


def _mem_kv(mem2, nw, w, tm):
    rows = mem2.shape[0]
    return pl.pallas_call(
        _mem_kv_body,
        grid=(rows // tm,),
        in_specs=[
            pl.BlockSpec((tm, D_MODEL), lambda i: (i, 0)),
            pl.BlockSpec((1, D_MODEL), lambda i: (0, 0)),
            pl.BlockSpec((D_MODEL, 2 * M_WIDTH), lambda i: (0, 0)),
        ],
        out_specs=pl.BlockSpec((tm, 2 * M_WIDTH), lambda i: (i, 0)),
        out_shape=jax.ShapeDtypeStruct((rows, 2 * M_WIDTH), BF16),
        compiler_params=_params(("parallel",)),
        name="mem_kv",
    )(mem2, nw, w)


def _mem_attn_body(q_ref, kv_ref, z_ref, o_ref):
    for h in range(M_HEADS):
        hs = slice(h * HEAD_DIM, (h + 1) * HEAD_DIM)
        s = _dot(q_ref[:, hs], kv_ref[:, hs], _NT)
        m = jnp.max(s, axis=-1, keepdims=True)
        p = jnp.exp(s - m)
        l = jnp.sum(p, axis=-1, keepdims=True)
        vh = kv_ref[:, M_WIDTH + h * HEAD_DIM:M_WIDTH + (h + 1) * HEAD_DIM]
        y = _dot(p.astype(BF16), vh) / l
        o_ref[:, hs] = (y * _silu(z_ref[:, hs].astype(F32))).astype(BF16)


def _mem_attn(mq, mkv, z, tq):
    b, seq, _ = mq.shape
    n_mem = mkv.shape[1]
    z_blk = (A_WIDTH + B_WIDTH) // M_WIDTH
    return pl.pallas_call(
        _mem_attn_body,
        grid=(b, seq // tq),
        in_specs=[
            pl.BlockSpec((None, tq, M_WIDTH), lambda bi, i: (bi, i, 0)),
            pl.BlockSpec((None, n_mem, 2 * M_WIDTH), lambda bi, i: (bi, 0, 0)),
            pl.BlockSpec((None, tq, M_WIDTH), lambda bi, i: (bi, i, z_blk)),
        ],
        out_specs=pl.BlockSpec((None, tq, M_WIDTH), lambda bi, i: (bi, i, 0)),
        out_shape=jax.ShapeDtypeStruct((b, seq, M_WIDTH), BF16),
        compiler_params=_params(("parallel", "parallel")),
        name="mem_attn",
    )(mq, mkv, z)


def _out_proj_body(ya_ref, of_ref, ob_ref, zb_ref, ym_ref, dnw_ref, w_ref, pw_ref, x_ref, o_ref):
    ob = of_ref[...].astype(F32) + ob_ref[...].astype(F32)
    dnw = dnw_ref[...]
    parts = []
    for h in range(B_HEADS):
        hs = slice(h * HEAD_DIM, (h + 1) * HEAD_DIM)
        oh = ob[:, hs]
        yn = oh * lax.rsqrt(jnp.mean(oh * oh, axis=-1, keepdims=True) + EPS) * dnw
        parts.append((yn * _silu(zb_ref[:, hs].astype(F32))).astype(BF16))
    yb = jnp.concatenate(parts, axis=1)
    acc = _dot(ya_ref[...], w_ref[0:A_WIDTH, :])
    acc = acc + _dot(yb, w_ref[A_WIDTH:A_WIDTH + B_WIDTH, :])
    acc = acc + _dot(ym_ref[...], w_ref[A_WIDTH + B_WIDTH:, :])
    y = acc * lax.rsqrt(jnp.mean(acc * acc, axis=-1, keepdims=True) + EPS) * pw_ref[...]
    o_ref[...] = x_ref[...] + y


def _out_proj(ya, o_f, o_b, z, ym, dnw, w, pw, x, tm):
    b, seq, _ = ya.shape
    z_blk = A_WIDTH // B_WIDTH
    row = lambda bi, i: (bi, i, 0)
    const = lambda bi, i: (0, 0)
    return pl.pallas_call(
        _out_proj_body,
        grid=(b, seq // tm),
        in_specs=[
            pl.BlockSpec((None, tm, A_WIDTH), row),
            pl.BlockSpec((None, tm, B_WIDTH), row),
            pl.BlockSpec((None, tm, B_WIDTH), row),
            pl.BlockSpec((None, tm, B_WIDTH), lambda bi, i: (bi, i, z_blk)),
            pl.BlockSpec((None, tm, M_WIDTH), row),
            pl.BlockSpec((1, HEAD_DIM), const),
            pl.BlockSpec((D_MIX, D_MODEL), const),
            pl.BlockSpec((1, D_MODEL), const),
            pl.BlockSpec((None, tm, D_MODEL), row),
        ],
        out_specs=pl.BlockSpec((None, tm, D_MODEL), row),
        out_shape=jax.ShapeDtypeStruct((b, seq, D_MODEL), F32),
        compiler_params=_params(("parallel", "parallel")),
        name="out_proj",
    )(ya, o_f, o_b, z, ym, dnw, w, pw, x)


def _rope_tables(seq):
    t = np.arange(seq)
    inv_freq = (np.float32(ROPE_THETA) ** (-np.arange(0, ROPE_AXIS_DIM, 2, dtype=np.float32) / ROPE_AXIS_DIM)
                ).astype(np.float32)
    ang_row = (t // GRID_W).astype(np.float32)[:, None] * inv_freq
    ang_col = (t % GRID_W).astype(np.float32)[:, None] * inv_freq
    cr, sr, cc, sc = np.cos(ang_row), np.sin(ang_row), np.cos(ang_col), np.sin(ang_col)
    return (jnp.asarray(np.concatenate([cr, cr, cc, cc], axis=-1), F32),
            jnp.asarray(np.concatenate([-sr, sr, -sc, sc], axis=-1), F32))


def _arrange_w_in(w):
    pts = np.cumsum([A_WIDTH, A_KV_WIDTH, A_KV_WIDTH, B_QKV_WIDTH, N_DIR * B_HEADS, N_DIR * B_HEADS, M_WIDTH])
    aq, ak, av, bqkv, ba, bb, mq, z = jnp.split(w, pts, axis=1)
    gate_blocks = []
    for d in range(N_DIR):
        sl = slice(d * B_HEADS, (d + 1) * B_HEADS)
        pad = jnp.zeros((w.shape[0], HEAD_DIM - 2 * B_HEADS), w.dtype)
        gate_blocks += [ba[:, sl], bb[:, sl], pad]
    return jnp.concatenate([aq, ak, av, bqkv] + gate_blocks + [mq, z], axis=1).astype(BF16)


def _gate_lane_vec(p):
    out = jnp.zeros((N_DIR, HEAD_DIM), F32).at[:, :B_HEADS].set(p.astype(F32))
    return out.reshape(1, GATE_WIDTH)


def kernel(x, mem, norm_pre_w, w_in, q_norm_w, k_norm_w, conv_w, a_log, dt_bias, delta_norm_w,
           mem_norm_w, w_mem_kv, w_out, norm_post_w):
    b, seq, _ = x.shape
    n_mem = mem.shape[1]
    assert seq % 512 == 0 and seq % GRID_W == 0 and w_in.shape[0] == 1
    l = 0
    tm = 512

    cos_t, sin_t = _rope_tables(seq)
    qkw = jnp.zeros((SUBLANES, HEAD_DIM), F32).at[0].set(q_norm_w[l]).at[1].set(k_norm_w[l])
    cw = jnp.zeros((SUBLANES, B_QKV_WIDTH), F32).at[:CONV_K].set(conv_w[l])
    aq, ak, av, qkvn, gates, mq, z = _in_proj(
        x.reshape(b * seq, D_MODEL), norm_pre_w[l].reshape(1, D_MODEL), _arrange_w_in(w_in[l]),
        qkw, cos_t, sin_t, cw, _gate_lane_vec(a_log[l]), _gate_lane_vec(dt_bias[l]), seq, tm)
    shp = lambda a: a.reshape(b, seq, a.shape[-1])
    ak, qkvn, gates, mq, z = map(shp, (ak, qkvn, gates, mq, z))

    ya = _attention(aq, ak, av, z)
    o_f, o_b = _gdn(qkvn, gates, gsz=8)

    mkv = _mem_kv(mem.reshape(b * n_mem, D_MODEL), mem_norm_w[l].reshape(1, D_MODEL),
                  w_mem_kv[l].astype(BF16), tm=256)
    ym = _mem_attn(mq, mkv.reshape(b, n_mem, 2 * M_WIDTH), z, tq=512)

    return _out_proj(ya, o_f, o_b, z, ym, delta_norm_w[l].reshape(1, HEAD_DIM), w_out[l].astype(BF16),
                     norm_post_w[l].reshape(1, D_MODEL), x, tm=512)
```

```python
import functools

import jax
import jax.numpy as jnp
import numpy as np
from jax import lax
from jax.experimental import pallas as pl
from jax.experimental.pallas import tpu as pltpu

F32 = jnp.float32
BF16 = jnp.bfloat16

D_MODEL = 1024
HEAD_DIM = 128
GRID_W = 64
A_HEADS = 8
A_KV_HEADS = 2
A_GROUP = A_HEADS // A_KV_HEADS
A_WIDTH = A_HEADS * HEAD_DIM
A_KV_WIDTH = A_KV_HEADS * HEAD_DIM
B_HEADS = 4
B_WIDTH = B_HEADS * HEAD_DIM
B_QKV_WIDTH = 3 * B_WIDTH
N_DIR = 2
CONV_K = 5
CHUNK = 64
M_HEADS = 4
M_WIDTH = M_HEADS * HEAD_DIM
D_MIX = A_WIDTH + B_WIDTH + M_WIDTH
ROPE_AXIS_DIM = HEAD_DIM // 2
ROPE_THETA = 10000.0
EPS = 1e-6
LOG2E = 1.4426950408889634
GDN_SKEW = 1
GATE_WIDTH = N_DIR * HEAD_DIM
QK_WIDTH = A_WIDTH + A_KV_WIDTH
SUBLANES = 8
BF16_ROWS = 2 * SUBLANES
V_ROWS = HEAD_DIM + BF16_ROWS
VMEM_LIMIT = 56 * 1024 * 1024

_NT = (((1,), (1,)), ((), ()))
_TN = (((0,), (0,)), ((), ()))


def _dot(a, b, dims=None, precision=None):
    if dims is None:
        return jnp.dot(a, b, preferred_element_type=F32, precision=precision)
    return lax.dot_general(a, b, dims, preferred_element_type=F32, precision=precision)


def _silu(x):
    return x * jax.nn.sigmoid(x)


def _params(sem):
    return pltpu.CompilerParams(dimension_semantics=sem, vmem_limit_bytes=VMEM_LIMIT)


def _in_proj_body(x_ref, xp_ref, xn_ref, nw_ref, w_ref, qkw_ref, cos_ref, sin_ref, cw_ref, alog_ref, dtb_ref,
                  aq_ref, ak_ref, av_ref, qkv_ref, gate_ref, mq_ref, z_ref, ext_ref, *, tm, n_seq_tiles):
    def pre_norm(x):
        ms = jnp.mean(x * x, axis=-1, keepdims=True)
        return (x * lax.rsqrt(ms + EPS) * nw_ref[...]).astype(BF16)

    h = pre_norm(x_ref[...])

    cos = cos_ref[...]
    sin = sin_ref[...]
    lane = lax.broadcasted_iota(jnp.int32, cos.shape, 1)
    first_half = (lane % (ROPE_AXIS_DIM)) < (ROPE_AXIS_DIM // 2)

    def norm_rope(y, w):
        yn = y * lax.rsqrt(jnp.mean(y * y, axis=-1, keepdims=True) + EPS) * w
        swapped = jnp.where(first_half,
                            pltpu.roll(yn, HEAD_DIM - ROPE_AXIS_DIM // 2, 1),
                            pltpu.roll(yn, ROPE_AXIS_DIM // 2, 1))
        return yn * cos + swapped * sin

    col_v = QK_WIDTH
    col_b = col_v + A_KV_WIDTH
    col_g = col_b + B_QKV_WIDTH
    col_m = col_g + GATE_WIDTH
    col_z = col_m + M_WIDTH

    ti = pl.program_id(0) % n_seq_tiles
    halo = BF16_ROWS
    h_ext = jnp.concatenate([pre_norm(xp_ref[...]), h, pre_norm(xn_ref[...])], axis=0)
    bw = 2 * HEAD_DIM

    def b_proj(c0):
        r = _dot(h_ext, w_ref[:, col_b + c0:col_b + c0 + bw])
        ext_ref[0:halo, c0:c0 + bw] = jnp.where(ti == 0, 0.0, r[0:halo])
        ext_ref[halo:halo + tm, c0:c0 + bw] = r[halo:halo + tm]
        ext_ref[halo + tm:, c0:c0 + bw] = jnp.where(ti == n_seq_tiles - 1, 0.0, r[halo + tm:])

    def conv_head(hh):
        cs = slice(hh * HEAD_DIM, (hh + 1) * HEAD_DIM)
        y = None
        for j in range(CONV_K):
            term = ext_ref[pl.ds(halo - CONV_K // 2 + j, tm), cs] * cw_ref[j:j + 1, cs]
            y = term if y is None else y + term
        y = _silu(y)
        if hh < 2 * B_HEADS:
            y = y * lax.rsqrt(jnp.sum(y * y, axis=-1, keepdims=True) + EPS)
        if hh < B_HEADS:
            y = y * (HEAD_DIM ** -0.5)
        qkv_ref[:, cs] = y.astype(BF16)

    def qk_pair(c0):
        y = _dot(h, w_ref[:, c0:c0 + 2 * HEAD_DIM])
        for j in range(2):
            hc = c0 + j * HEAD_DIM
            yh = y[:, j * HEAD_DIM:(j + 1) * HEAD_DIM]
            if hc < A_WIDTH:
                out = norm_rope(yh, qkw_ref[0:1, :]) * (HEAD_DIM ** -0.5 * LOG2E)
                aq_ref[hc:hc + HEAD_DIM, :] = out.T.astype(BF16)
            else:
                out = norm_rope(yh, qkw_ref[1:2, :])
                ak_ref[:, hc - A_WIDTH:hc - A_WIDTH + HEAD_DIM] = out.astype(BF16)

    def v_block():
        vt = _dot(h, w_ref[:, col_v:col_v + A_KV_WIDTH]).T.astype(BF16)
        pad_row = lax.broadcasted_iota(jnp.int32, (V_ROWS - HEAD_DIM, vt.shape[1]), 0)
        ones_pad = jnp.where(pad_row == 0, 1.0, 0.0).astype(BF16)
        for kv in range(A_KV_HEADS):
            av_ref[kv * V_ROWS:kv * V_ROWS + HEAD_DIM, :] = vt[kv * HEAD_DIM:(kv + 1) * HEAD_DIM]
            av_ref[kv * V_ROWS + HEAD_DIM:(kv + 1) * V_ROWS, :] = ones_pad

    def gate_block():
        g = _dot(h, w_ref[:, col_g:col_g + GATE_WIDTH])
        glane = lax.broadcasted_iota(jnp.int32, g.shape, 1) % HEAD_DIM
        t = g + dtb_ref[...]
        softplus = jnp.maximum(t, 0.0) + jnp.log1p(jnp.exp(-jnp.abs(t)))
        glog = -jnp.exp(alog_ref[...]) * softplus
        beta = jax.nn.sigmoid(g)
        gate_ref[...] = jnp.where(glane < B_HEADS, glog, jnp.where(glane < 2 * B_HEADS, beta, 0.0))

    def mq_block():
        mq_ref[...] = (_dot(h, w_ref[:, col_m:col_m + M_WIDTH]) * (HEAD_DIM ** -0.5)).astype(BF16)

    def z_block(c0):
        z_ref[:, c0:c0 + 512] = _dot(h, w_ref[:, col_z + c0:col_z + c0 + 512]).astype(BF16)

    for c0 in range(0, QK_WIDTH, 2 * HEAD_DIM):
        qk_pair(c0)
    v_block()
    for c0 in range(0, B_QKV_WIDTH, bw):
        b_proj(c0)
    for hh in range(B_QKV_WIDTH // HEAD_DIM):
        conv_head(hh)
    gate_block()
    mq_block()
    for c0 in range(0, D_MIX, 512):
        z_block(c0)


def _in_proj(x2, nw, w_all, qkw, cos_t, sin_t, cw, alog_v, dtb_v, seq, tm):
    rows = x2.shape[0]
    n_seq_tiles = seq // tm
    wtot = w_all.shape[1]
    row = lambda i: (i, 0)
    const = lambda i: (0, 0)
    nb = rows // seq
    halo = BF16_ROWS
    per = tm // halo
    last_blk = rows // halo - 1
    outs = [
        (A_KV_WIDTH, BF16), (B_QKV_WIDTH, BF16), (GATE_WIDTH, F32), (M_WIDTH, BF16), (D_MIX, BF16),
    ]
    t_specs = [
        pl.BlockSpec((None, None, A_WIDTH, tm), lambda i: (i // n_seq_tiles, i % n_seq_tiles, 0, 0)),
        pl.BlockSpec((None, None, A_KV_HEADS * V_ROWS, tm), lambda i: (i // n_seq_tiles, i % n_seq_tiles, 0, 0)),
    ]
    t_shapes = [
        jax.ShapeDtypeStruct((nb, n_seq_tiles, A_WIDTH, tm), BF16),
        jax.ShapeDtypeStruct((nb, n_seq_tiles, A_KV_HEADS * V_ROWS, tm), BF16),
    ]
    return pl.pallas_call(
        functools.partial(_in_proj_body, tm=tm, n_seq_tiles=n_seq_tiles),
        grid=(rows // tm,),
        in_specs=[
            pl.BlockSpec((tm, D_MODEL), row),
            pl.BlockSpec((halo, D_MODEL), lambda i: (jnp.maximum(i * per - 1, 0), 0)),
            pl.BlockSpec((halo, D_MODEL), lambda i: (jnp.minimum((i + 1) * per, last_blk), 0)),
            pl.BlockSpec((1, D_MODEL), const),
            pl.BlockSpec((D_MODEL, wtot), const, pipeline_mode=pl.Buffered(1)),
            pl.BlockSpec((SUBLANES, HEAD_DIM), const),
            pl.BlockSpec((tm, HEAD_DIM), lambda i: (i % n_seq_tiles, 0)),
            pl.BlockSpec((tm, HEAD_DIM), lambda i: (i % n_seq_tiles, 0)),
            pl.BlockSpec((SUBLANES, B_QKV_WIDTH), const),
            pl.BlockSpec((1, GATE_WIDTH), const),
            pl.BlockSpec((1, GATE_WIDTH), const),
        ],
        out_specs=[t_specs[0], pl.BlockSpec((tm, outs[0][0]), row), t_specs[1]]
        + [pl.BlockSpec((tm, w), row) for w, _ in outs[1:]],
        out_shape=[t_shapes[0], jax.ShapeDtypeStruct((rows, outs[0][0]), outs[0][1]), t_shapes[1]]
        + [jax.ShapeDtypeStruct((rows, w), dt) for w, dt in outs[1:]],
        scratch_shapes=[pltpu.VMEM((tm + 2 * halo, B_QKV_WIDTH), F32)],
        compiler_params=_params(("parallel",)),
        name="in_proj",
    )(x2, x2, x2, nw, w_all, qkw, cos_t, sin_t, cw, alog_v, dtb_v)


def _attn_body(qt_ref, k_ref, vt_ref, z_ref, o_ref, m_ref, acc_ref, s_ref, p_ref, *, n_kv, n_q, tq):
    def scores(qi, j, slot, g):
        s_ref[slot, g] = _dot(k_ref[j], qt_ref[qi, g * HEAD_DIM:(g + 1) * HEAD_DIM, :])

    tk = s_ref.shape[2]
    sub = SUBLANES
    rc = 4 * sub

    def softmax_pv(j, slot, g):
        mx = s_ref[slot, g, 0:sub, :]
        for c in range(1, tk // sub):
            mx = jnp.maximum(mx, s_ref[slot, g, c * sub:(c + 1) * sub, :])
        m_new = jnp.max(mx, axis=0, keepdims=True)
        if j > 0:
            m_old = m_ref[g]
            m_new = jnp.maximum(m_old, m_new)
            alpha = jnp.exp2(m_old - m_new)
        m_ref[g] = m_new
        for c in range(tk // rc):
            pc = jnp.exp2(s_ref[slot, g, c * rc:(c + 1) * rc, :] - m_new)
            p_ref[slot, g, c * rc:(c + 1) * rc, :] = pc.astype(BF16)
        pv = _dot(vt_ref[j], p_ref[slot, g])
        if j == 0:
            acc_ref[g] = pv
        else:
            acc_ref[g] = alpha * acc_ref[g] + pv

    for g in range(A_GROUP):
        scores(0, 0, 0, g)

    def q_tile(qi, carry):
        for j in range(n_kv):
            slot = j % 2
            for g in range(A_GROUP):
                if j + 1 < n_kv:
                    scores(qi, j + 1, 1 - slot, g)
                else:
                    scores(jnp.minimum(qi + 1, n_q - 1), 0, 1 - slot, g)
                softmax_pv(j, slot, g)
        rows = pl.ds(pl.multiple_of(qi * tq, tq), tq)
        for g in range(A_GROUP):
            hs = slice(g * HEAD_DIM, (g + 1) * HEAD_DIM)
            y = (acc_ref[g, 0:HEAD_DIM, :] / acc_ref[g, HEAD_DIM:HEAD_DIM + 1, :]).T
            o_ref[rows, hs] = (y * _silu(z_ref[rows, hs].astype(F32))).astype(BF16)
        return carry

    lax.fori_loop(0, n_q, q_tile, 0)


def _attention(aqt, ak, avt, z):
    b, n_q, _, tq = aqt.shape
    n_kv, tk = avt.shape[1], avt.shape[3]
    seq = n_kv * tk
    assert n_kv % 2 == 0
    gw = A_GROUP * HEAD_DIM
    akr = ak.reshape(b, n_kv, tk, A_KV_WIDTH)
    return pl.pallas_call(
        functools.partial(_attn_body, n_kv=n_kv, n_q=n_q, tq=tq),
        grid=(b, A_KV_HEADS),
        in_specs=[
            pl.BlockSpec((None, n_q, gw, tq), lambda bi, kv: (bi, 0, kv, 0)),
            pl.BlockSpec((None, n_kv, tk, HEAD_DIM), lambda bi, kv: (bi, 0, 0, kv)),
            pl.BlockSpec((None, n_kv, V_ROWS, tk), lambda bi, kv: (bi, 0, kv, 0)),
            pl.BlockSpec((None, seq, gw), lambda bi, kv: (bi, 0, kv)),
        ],
        out_specs=pl.BlockSpec((None, seq, gw), lambda bi, kv: (bi, 0, kv)),
        out_shape=jax.ShapeDtypeStruct((b, seq, A_WIDTH), BF16),
        scratch_shapes=[pltpu.VMEM((A_GROUP, 1, tq), F32), pltpu.VMEM((A_GROUP, V_ROWS, tq), F32), pltpu.VMEM((2, A_GROUP, tk, tq), F32),
                        pltpu.VMEM((2, A_GROUP, tk, tq), BF16)],
        compiler_params=_params(("parallel", "arbitrary")),
        name="attn",
    )(aqt, akr, avt, z)


def _split2(a):
    hi = a.astype(BF16)
    lo = (a - hi.astype(F32)).astype(BF16)
    return [hi, lo]


def _lane_blocks(cols, width, block):
    rows = cols[0].shape[0]
    blk = lax.broadcasted_iota(jnp.int32, (rows, width), 1) // block
    out = jnp.broadcast_to(cols[-1], (rows, width))
    for i in range(len(cols) - 2, -1, -1):
        out = jnp.where(blk == i, cols[i], out)
    return out


def _tile_rows(x, n):
    return jnp.concatenate([x] * n, axis=0)


def _gdn_body(qf_ref, kf_ref, vf_ref, gf_ref, qb_ref, kb_ref, vb_ref, gb_ref, of_ref, ob_ref,
              s_ref, mq_ref, bo_ref, dc_ref, *, gsz, nb):
    c_, h_, w_ = CHUNK, B_HEADS, B_HEADS * CHUNK
    n_pair = h_ // 2
    pw = 2 * HEAD_DIM
    step = pl.program_id(0)

    @pl.when(step == 0)
    def _():
        s_ref[...] = jnp.zeros_like(s_ref)
        mq_ref[...] = jnp.zeros_like(mq_ref)
        bo_ref[...] = jnp.zeros_like(bo_ref)
        dc_ref[...] = jnp.zeros_like(dc_ref)

    keep = jnp.where((step - 1) % nb == 0, 0.0, 1.0)
    o_refs = (of_ref, ob_ref)
    zero_blk = jnp.zeros((HEAD_DIM, HEAD_DIM), BF16)
    for g in range(gsz):
        for d in range(N_DIR):
            rows = g if d == 0 else gsz - 1 - g
            for pr in range(n_pair):
                state = s_ref[d * n_pair + pr]
                if g == 0:
                    state = state * keep
                sb = state.astype(BF16)
                s_bd = jnp.concatenate([jnp.concatenate([sb[:, :HEAD_DIM], zero_blk], axis=1),
                                        jnp.concatenate([zero_blk, sb[:, HEAD_DIM:]], axis=1)], axis=0)
                r = _dot(mq_ref[g, d, pr], s_bd)
                bo = bo_ref[g, d, pr]
                s_ref[d * n_pair + pr] = (state * dc_ref[g, d, 0:1, pr * pw:(pr + 1) * pw]
                                          + r[:HEAD_DIM] + bo[:HEAD_DIM])
                o_refs[d][rows * c_:(rows + 1) * c_, pr * pw:(pr + 1) * pw] = (
                    r[HEAD_DIM:] + bo[HEAD_DIM:]).astype(BF16)

    ri = lax.broadcasted_iota(jnp.int32, (c_, w_), 0)
    jj = lax.broadcasted_iota(jnp.int32, (c_, w_), 1) % c_
    eye_w = (ri == jj).astype(F32)
    r2 = lax.broadcasted_iota(jnp.int32, (c_, c_), 0)
    c2 = lax.broadcasted_iota(jnp.int32, (c_, c_), 1)

    def blockdiag_mask(width, lanes_per_head, period=None):
        rh = lax.broadcasted_iota(jnp.int32, (w_, width), 0) // c_
        ln = lax.broadcasted_iota(jnp.int32, (w_, width), 1)
        if period is not None:
            ln = ln % period
        return (rh == ln // lanes_per_head).astype(BF16)

    bd_mask = blockdiag_mask(w_, c_)
    bdk_mask = blockdiag_mask(B_WIDTH, HEAD_DIM)
    bdr_mask = blockdiag_mask(2 * B_WIDTH, HEAD_DIM, period=B_WIDTH)

    dirs = []
    for d in range(N_DIR):
        diff = (ri - jj) if d == 0 else (jj - ri)
        d2 = (r2 - c2) if d == 0 else (c2 - r2)
        dirs.append(dict(incl=diff >= 0, strict=diff > 0, m2=(diff > 0).astype(F32),
                         m1=(d2 >= 0).astype(BF16)))
    row8 = lax.broadcasted_iota(jnp.int32, (SUBLANES, HEAD_DIM), 0)
    srcs = ((qf_ref, kf_ref, vf_ref, gf_ref), (qb_ref, kb_ref, vb_ref, gb_ref))

    def st_scores(t):
        q_ref, k_ref, v_ref, g_ref = srcs[t["d"]]
        pos = t["g"] if t["d"] == 0 else gsz - 1 - t["g"]
        sl = slice(pos * c_, (pos + 1) * c_)
        kc = k_ref[sl, :]
        kq = _dot(jnp.concatenate([kc, q_ref[sl, :]], axis=0), _tile_rows(kc, h_) * bdk_mask, _NT)
        t.update(m=dirs[t["d"]], sl=sl, q_ref=q_ref, k_ref=k_ref, v_ref=v_ref, g_ref=g_ref,
                 kk_w=kq[:c_], qk_w=kq[c_:])

    def st_decay(t):
        m = t["m"]
        gt = t["g_ref"][t["sl"], :]
        gl_w = _lane_blocks([gt[:, h:h + 1] for h in range(h_)], w_, c_)
        d3 = _dot(m["m1"], jnp.concatenate(_split2(gl_w * m["m2"]) + _split2(gt), axis=1))
        dm_w = d3[:, :w_] + d3[:, w_:2 * w_]
        o3 = 2 * w_
        t["gcum"] = d3[:, o3:o3 + HEAD_DIM] + d3[:, o3 + HEAD_DIM:]
        t["gt"] = gt
        t["decay"] = jnp.exp(dm_w)
        beta_w = _lane_blocks([gt[:, h_ + h:h_ + h + 1] for h in range(h_)], w_, c_)
        t["x"] = -jnp.where(m["strict"], t["kk_w"] * beta_w * t["decay"], 0.0)

    def st_inv_first(t):
        xb = t["x"].astype(BF16)
        t["xp"] = _dot(xb, _tile_rows(xb, h_) * bd_mask)
        t["p"] = eye_w + t["x"]

    def st_inv_level(t):
        xpb = t["xp"].astype(BF16)
        r = _dot(jnp.concatenate([t["p"].astype(BF16), xpb], axis=0), _tile_rows(xpb, h_) * bd_mask)
        t["p"] = t["p"] + r[:c_]
        t["xp"] = r[c_:]

    def st_inv_last(t):
        t["p"] = t["p"] + _dot(t["p"].astype(BF16), _tile_rows(t["xp"].astype(BF16), h_) * bd_mask)

    def st_uw(t):
        gt, sl = t["gt"], t["sl"]
        kf = t["k_ref"][sl, :].astype(F32)
        vf = t["v_ref"][sl, :].astype(F32)
        g_hd = _lane_blocks([t["gcum"][:, h:h + 1] for h in range(h_)], B_WIDTH, HEAD_DIM)
        beta_hd = _lane_blocks([gt[:, h_ + h:h_ + h + 1] for h in range(h_)], B_WIDTH, HEAD_DIM)
        gtot = jnp.sum(gt, axis=0, keepdims=True)
        gtot_hd = _lane_blocks([gtot[:, h:h + 1] for h in range(h_)], B_WIDTH, HEAD_DIM)
        t["eg_hd"] = jnp.exp(g_hd)
        t["dc_hd"] = jnp.exp(gtot_hd)
        t["k_tail"] = (kf * jnp.exp(gtot_hd - g_hd)).astype(BF16)
        rhs = jnp.concatenate([vf * beta_hd, kf * (beta_hd * t["eg_hd"])], axis=1).astype(BF16)
        t["uw"] = _dot(t["p"].astype(BF16), _tile_rows(rhs, h_) * bdr_mask)

    def st_intra(t):
        intra = jnp.where(t["m"]["incl"], t["qk_w"] * t["decay"], 0.0).astype(BF16)
        t["iuw"] = _dot(intra, _tile_rows(t["uw"].astype(BF16), h_) * bdr_mask)

    def st_mb(t):
        uw = t["uw"]
        t["mb"] = []
        for h in range(h_):
            hs = slice(h * HEAD_DIM, (h + 1) * HEAD_DIM)
            wu = jnp.concatenate([-uw[:, B_WIDTH + h * HEAD_DIM:B_WIDTH + (h + 1) * HEAD_DIM], uw[:, hs]],
                                 axis=1).astype(BF16)
            t["mb"].append(_dot(t["k_tail"][:, hs], wu, _TN))

    def st_store(t):
        g, d = t["g"], t["d"]
        qf = t["q_ref"][t["sl"], :].astype(F32)
        o_local = t["iuw"][:, :B_WIDTH]
        q_eff = (qf * t["eg_hd"] - t["iuw"][:, B_WIDTH:]).astype(BF16)
        for h in range(h_):
            pr, ls = h // 2, slice((h % 2) * HEAD_DIM, (h % 2 + 1) * HEAD_DIM)
            mb = t["mb"][h]
            mq_ref[g, d, pr, 0:HEAD_DIM, ls] = mb[:, :HEAD_DIM].astype(BF16)
            bo_ref[g, d, pr, 0:HEAD_DIM, ls] = mb[:, HEAD_DIM:]
        for pr in range(n_pair):
            ps = slice(pr * pw, (pr + 1) * pw)
            mq_ref[g, d, pr, HEAD_DIM:, :] = q_eff[:, ps]
            bo_ref[g, d, pr, HEAD_DIM:, :] = o_local[:, ps]
        dc_ref[g, d] = jnp.broadcast_to(t["dc_hd"], (SUBLANES, B_WIDTH))

    stages = ([st_scores, st_decay, st_inv_first] + [st_inv_level] * 4
              + [st_inv_last, st_uw, st_intra, st_mb, st_store])
    inst = [dict(g=g, d=d) for g in range(gsz) for d in range(N_DIR)]
    half = len(inst) // 2
    groups = (inst[:half], inst[half:])
    for k in range(len(stages) + GDN_SKEW):
        for idx in range(half):
            if k < len(stages):
                stages[k](groups[0][idx])
            if 0 <= k - GDN_SKEW < len(stages):
                stages[k - GDN_SKEW](groups[1][idx])


def _gdn(qkvn, gates, gsz):
    b, seq, _ = qkvn.shape
    nb = seq // (gsz * CHUNK)
    rows = gsz * CHUNK
    mrows = HEAD_DIM + CHUNK

    total = b * nb
    n_pair = B_HEADS // 2

    def cur(t):
        return jnp.minimum(t, total - 1)

    def prev(t):
        return jnp.maximum(t - 1, 0)

    def fwd_in(j):
        return lambda t: (cur(t) // nb, cur(t) % nb, j)

    def bwd_in(j):
        return lambda t: (cur(t) // nb, nb - 1 - cur(t) % nb, j)

    out = jax.ShapeDtypeStruct((b, seq, B_WIDTH), BF16)
    return pl.pallas_call(
        functools.partial(_gdn_body, gsz=gsz, nb=nb),
        grid=(total + 1,),
        in_specs=[pl.BlockSpec((None, rows, B_WIDTH), fwd_in(j)) for j in range(3)]
        + [pl.BlockSpec((None, rows, HEAD_DIM), fwd_in(0))]
        + [pl.BlockSpec((None, rows, B_WIDTH), bwd_in(j)) for j in range(3)]
        + [pl.BlockSpec((None, rows, HEAD_DIM), bwd_in(1))],
        out_specs=[
            pl.BlockSpec((None, rows, B_WIDTH), lambda t: (prev(t) // nb, prev(t) % nb, 0)),
            pl.BlockSpec((None, rows, B_WIDTH), lambda t: (prev(t) // nb, nb - 1 - prev(t) % nb, 0)),
        ],
        out_shape=[out, out],
        scratch_shapes=[
            pltpu.VMEM((N_DIR * n_pair, HEAD_DIM, 2 * HEAD_DIM), F32),
            pltpu.VMEM((gsz, N_DIR, n_pair, mrows, 2 * HEAD_DIM), BF16),
            pltpu.VMEM((gsz, N_DIR, n_pair, mrows, 2 * HEAD_DIM), F32),
            pltpu.VMEM((gsz, N_DIR, SUBLANES, B_WIDTH), F32),
        ],
        compiler_params=_params(("arbitrary",)),
        name="gdn",
    )(qkvn, qkvn, qkvn, gates, qkvn, qkvn, qkvn, gates)


def _mem_kv_body(m_ref, nw_ref, w_ref, o_ref):
    x = m_ref[...]
    ms = jnp.mean(x * x, axis=-1, keepdims=True)
    h = (x * lax.rsqrt(ms + EPS) * nw_ref[...]).astype(BF16)
    o_ref[...] = _dot(h, w_ref[...]).astype(BF16)


def _mem_kv(mem2, nw, w, tm):
    rows = mem2.shape[0]
    return pl.pallas_call(
        _mem_kv_body,
        grid=(rows // tm,),
        in_specs=[
            pl.BlockSpec((tm, D_MODEL), lambda i: (i, 0)),
            pl.BlockSpec((1, D_MODEL), lambda i: (0, 0)),
            pl.BlockSpec((D_MODEL, 2 * M_WIDTH), lambda i: (0, 0)),
        ],
        out_specs=pl.BlockSpec((tm, 2 * M_WIDTH), lambda i: (i, 0)),
        out_shape=jax.ShapeDtypeStruct((rows, 2 * M_WIDTH), BF16),
        compiler_params=_params(("parallel",)),
        name="mem_kv",
    )(mem2, nw, w)


def _out_proj_body(ya_ref, of_ref, ob_ref, z_ref, mq_ref, kv_ref, dnw_ref, w_ref, pw_ref, x_ref, o_ref):
    ob = of_ref[...].astype(F32) + ob_ref[...].astype(F32)
    dnw = dnw_ref[...]
    parts = []
    for h in range(B_HEADS):
        hs = slice(h * HEAD_DIM, (h + 1) * HEAD_DIM)
        oh = ob[:, hs]
        yn = oh * lax.rsqrt(jnp.mean(oh * oh, axis=-1, keepdims=True) + EPS) * dnw
        parts.append((yn * _silu(z_ref[:, hs].astype(F32))).astype(BF16))
    yb = jnp.concatenate(parts, axis=1)
    parts = []
    for h in range(M_HEADS):
        hs = slice(h * HEAD_DIM, (h + 1) * HEAD_DIM)
        s = _dot(mq_ref[:, hs], kv_ref[:, hs], _NT)
        p = jnp.exp(s - jnp.max(s, axis=-1, keepdims=True))
        l = jnp.sum(p, axis=-1, keepdims=True)
        y = _dot(p.astype(BF16), kv_ref[:, M_WIDTH + h * HEAD_DIM:M_WIDTH + (h + 1) * HEAD_DIM]) / l
        zs = slice(B_WIDTH + h * HEAD_DIM, B_WIDTH + (h + 1) * HEAD_DIM)
        parts.append((y * _silu(z_ref[:, zs].astype(F32))).astype(BF16))
    ym = jnp.concatenate(parts, axis=1)
    acc = _dot(ya_ref[...], w_ref[0:A_WIDTH, :])
    acc = acc + _dot(yb, w_ref[A_WIDTH:A_WIDTH + B_WIDTH, :])
    acc = acc + _dot(ym, w_ref[A_WIDTH + B_WIDTH:, :])
    y = acc * lax.rsqrt(jnp.mean(acc * acc, axis=-1, keepdims=True) + EPS) * pw_ref[...]
    o_ref[...] = x_ref[...] + y


def _out_proj(ya, o_f, o_b, z, mq, mkv, dnw, w, pw, x, tm):
    b, seq, _ = ya.shape
    n_mem = mkv.shape[1]
    zw = B_WIDTH + M_WIDTH
    assert A_WIDTH % zw == 0
    row = lambda bi, i: (bi, i, 0)
    const = lambda bi, i: (0, 0)
    return pl.pallas_call(
        _out_proj_body,
        grid=(b, seq // tm),
        in_specs=[
            pl.BlockSpec((None, tm, A_WIDTH), row),
            pl.BlockSpec((None, tm, B_WIDTH), row),
            pl.BlockSpec((None, tm, B_WIDTH), row),
            pl.BlockSpec((None, tm, zw), lambda bi, i: (bi, i, A_WIDTH // zw)),
            pl.BlockSpec((None, tm, M_WIDTH), row),
            pl.BlockSpec((None, n_mem, 2 * M_WIDTH), lambda bi, i: (bi, 0, 0)),
            pl.BlockSpec((1, HEAD_DIM), const),
            pl.BlockSpec((D_MIX, D_MODEL), const),
            pl.BlockSpec((1, D_MODEL), const),
            pl.BlockSpec((None, tm, D_MODEL), row),
        ],
        out_specs=pl.BlockSpec((None, tm, D_MODEL), row),
        out_shape=jax.ShapeDtypeStruct((b, seq, D_MODEL), F32),
        compiler_params=_params(("parallel", "parallel")),
        name="out_proj",
    )(ya, o_f, o_b, z, mq, mkv, dnw, w, pw, x)


def _rope_tables(seq):
    t = np.arange(seq)
    inv_freq = (np.float32(ROPE_THETA) ** (-np.arange(0, ROPE_AXIS_DIM, 2, dtype=np.float32) / ROPE_AXIS_DIM)
                ).astype(np.float32)
    ang_row = (t // GRID_W).astype(np.float32)[:, None] * inv_freq
    ang_col = (t % GRID_W).astype(np.float32)[:, None] * inv_freq
    cr, sr, cc, sc = np.cos(ang_row), np.sin(ang_row), np.cos(ang_col), np.sin(ang_col)
    return (jnp.asarray(np.concatenate([cr, cr, cc, cc], axis=-1), F32),
            jnp.asarray(np.concatenate([-sr, sr, -sc, sc], axis=-1), F32))


def _arrange_w_in(w):
    pts = np.cumsum([A_WIDTH, A_KV_WIDTH, A_KV_WIDTH, B_QKV_WIDTH, N_DIR * B_HEADS, N_DIR * B_HEADS, M_WIDTH])
    aq, ak, av, bqkv, ba, bb, mq, z = jnp.split(w, pts, axis=1)
    gate_blocks = []
    for d in range(N_DIR):
        sl = slice(d * B_HEADS, (d + 1) * B_HEADS)
        pad = jnp.zeros((w.shape[0], HEAD_DIM - 2 * B_HEADS), w.dtype)
        gate_blocks += [ba[:, sl], bb[:, sl], pad]
    return jnp.concatenate([aq, ak, av, bqkv] + gate_blocks + [mq, z], axis=1).astype(BF16)


def _gate_lane_vec(p):
    out = jnp.zeros((N_DIR, HEAD_DIM), F32).at[:, :B_HEADS].set(p.astype(F32))
    return out.reshape(1, GATE_WIDTH)


def kernel(x, mem, norm_pre_w, w_in, q_norm_w, k_norm_w, conv_w, a_log, dt_bias, delta_norm_w,
           mem_norm_w, w_mem_kv, w_out, norm_post_w):
    b, seq, _ = x.shape
    n_mem = mem.shape[1]
    assert seq % 512 == 0 and seq % GRID_W == 0 and w_in.shape[0] == 1
    l = 0
    tm = 512

    cos_t, sin_t = _rope_tables(seq)
    qkw = jnp.zeros((SUBLANES, HEAD_DIM), F32).at[0].set(q_norm_w[l]).at[1].set(k_norm_w[l])
    cw = jnp.zeros((SUBLANES, B_QKV_WIDTH), F32).at[:CONV_K].set(conv_w[l])
    aq, ak, av, qkvn, gates, mq, z = _in_proj(
        x.reshape(b * seq, D_MODEL), norm_pre_w[l].reshape(1, D_MODEL), _arrange_w_in(w_in[l]),
        qkw, cos_t, sin_t, cw, _gate_lane_vec(a_log[l]), _gate_lane_vec(dt_bias[l]), seq, tm)
    shp = lambda a: a.reshape(b, seq, a.shape[-1])
    ak, qkvn, gates, mq, z = map(shp, (ak, qkvn, gates, mq, z))

    ya = _attention(aq, ak, av, z)
    o_f, o_b = _gdn(qkvn, gates, gsz=8)

    mkv = _mem_kv(mem.reshape(b * n_mem, D_MODEL), mem_norm_w[l].reshape(1, D_MODEL),
                  w_mem_kv[l].astype(BF16), tm=256)

    return _out_proj(ya, o_f, o_b, z, mq, mkv.reshape(b, n_mem, 2 * M_WIDTH),
                     delta_norm_w[l].reshape(1, HEAD_DIM), w_out[l].astype(BF16),
                     norm_post_w[l].reshape(1, D_MODEL), x, tm=512)
```

```python
import functools

import jax
import jax.numpy as jnp
import numpy as np
from jax import lax
from jax.experimental import pallas as pl
from jax.experimental.pallas import tpu as pltpu

F32 = jnp.float32
BF16 = jnp.bfloat16

D_MODEL = 1024
HEAD_DIM = 128
GRID_W = 64
A_HEADS = 8
A_KV_HEADS = 2
A_GROUP = A_HEADS // A_KV_HEADS
A_WIDTH = A_HEADS * HEAD_DIM
A_KV_WIDTH = A_KV_HEADS * HEAD_DIM
B_HEADS = 4
B_WIDTH = B_HEADS * HEAD_DIM
B_QKV_WIDTH = 3 * B_WIDTH
N_DIR = 2
CONV_K = 5
CHUNK = 64
M_HEADS = 4
M_WIDTH = M_HEADS * HEAD_DIM
D_MIX = A_WIDTH + B_WIDTH + M_WIDTH
ROPE_AXIS_DIM = HEAD_DIM // 2
ROPE_THETA = 10000.0
EPS = 1e-6
LOG2E = 1.4426950408889634
GDN_SKEW = 1
GATE_WIDTH = N_DIR * HEAD_DIM
QK_WIDTH = A_WIDTH + A_KV_WIDTH
HEAD_COLS = QK_WIDTH + A_KV_WIDTH + B_QKV_WIDTH
SUBLANES = 8
BF16_ROWS = 2 * SUBLANES
V_ROWS = HEAD_DIM + BF16_ROWS
VMEM_LIMIT = 56 * 1024 * 1024

_NT = (((1,), (1,)), ((), ()))
_TN = (((0,), (0,)), ((), ()))


def _dot(a, b, dims=None, precision=None):
    if dims is None:
        return jnp.dot(a, b, preferred_element_type=F32, precision=precision)
    return lax.dot_general(a, b, dims, preferred_element_type=F32, precision=precision)


def _silu(x):
    return x * jax.nn.sigmoid(x)


def _params(sem):
    return pltpu.CompilerParams(dimension_semantics=sem, vmem_limit_bytes=VMEM_LIMIT)


def _in_proj_body(x_ref, xp_ref, xn_ref, nw_ref, w_ref, wt_ref, qkw_ref, cos_ref, sin_ref, cw_ref, alog_ref,
                  dtb_ref, aq_ref, ak_ref, av_ref, qkv_ref, gate_ref, mq_ref, z_ref, ext_ref, *, tm, n_seq_tiles):
    def pre_norm(x):
        ms = jnp.mean(x * x, axis=-1, keepdims=True)
        return (x * lax.rsqrt(ms + EPS) * nw_ref[...]).astype(BF16)

    h = pre_norm(x_ref[...])

    cos = cos_ref[...]
    sin = sin_ref[...]
    lane = lax.broadcasted_iota(jnp.int32, cos.shape, 1)
    first_half = (lane % (ROPE_AXIS_DIM)) < (ROPE_AXIS_DIM // 2)

    def norm_rope(y, w):
        yn = y * lax.rsqrt(jnp.mean(y * y, axis=-1, keepdims=True) + EPS) * w
        swapped = jnp.where(first_half,
                            pltpu.roll(yn, HEAD_DIM - ROPE_AXIS_DIM // 2, 1),
                            pltpu.roll(yn, ROPE_AXIS_DIM // 2, 1))
        return yn * cos + swapped * sin

    col_v = QK_WIDTH
    col_b = col_v + A_KV_WIDTH
    col_g = 0
    col_m = col_g + GATE_WIDTH
    col_z = col_m + M_WIDTH

    ti = pl.program_id(0) % n_seq_tiles
    halo = BF16_ROWS
    h_ext = jnp.concatenate([pre_norm(xp_ref[...]), h, pre_norm(xn_ref[...])], axis=0)
    bw = 2 * HEAD_DIM

    def b_proj(c0):
        r = _dot(h_ext, w_ref[:, col_b + c0:col_b + c0 + bw])
        ext_ref[0:halo, c0:c0 + bw] = jnp.where(ti == 0, 0.0, r[0:halo])
        ext_ref[halo:halo + tm, c0:c0 + bw] = r[halo:halo + tm]
        ext_ref[halo + tm:, c0:c0 + bw] = jnp.where(ti == n_seq_tiles - 1, 0.0, r[halo + tm:])

    def conv_head(hh):
        cs = slice(hh * HEAD_DIM, (hh + 1) * HEAD_DIM)
        xe = ext_ref[:, cs]
        n_ext = xe.shape[0]
        y = None
        for j in range(CONV_K):
            sh = (CONV_K // 2 - j) % n_ext
            xs = xe if sh == 0 else pltpu.roll(xe, sh, 0)
            term = xs[halo:halo + tm] * cw_ref[j:j + 1, cs]
            y = term if y is None else y + term
        y = _silu(y)
        if hh < 2 * B_HEADS:
            y = y * lax.rsqrt(jnp.sum(y * y, axis=-1, keepdims=True) + EPS)
        if hh < B_HEADS:
            y = y * (HEAD_DIM ** -0.5)
        qkv_ref[:, cs] = y.astype(BF16)

    def qk_pair(c0):
        y = _dot(h, w_ref[:, c0:c0 + 2 * HEAD_DIM])
        for j in range(2):
            hc = c0 + j * HEAD_DIM
            yh = y[:, j * HEAD_DIM:(j + 1) * HEAD_DIM]
            if hc < A_WIDTH:
                out = norm_rope(yh, qkw_ref[0:1, :]) * (HEAD_DIM ** -0.5 * LOG2E)
                aq_ref[hc:hc + HEAD_DIM, :] = out.T.astype(BF16)
            else:
                out = norm_rope(yh, qkw_ref[1:2, :])
                ak_ref[:, hc - A_WIDTH:hc - A_WIDTH + HEAD_DIM] = out.astype(BF16)

    def v_block():
        vt = _dot(h, w_ref[:, col_v:col_v + A_KV_WIDTH]).T.astype(BF16)
        pad_row = lax.broadcasted_iota(jnp.int32, (V_ROWS - HEAD_DIM, vt.shape[1]), 0)
        ones_pad = jnp.where(pad_row == 0, 1.0, 0.0).astype(BF16)
        for kv in range(A_KV_HEADS):
            av_ref[kv * V_ROWS:kv * V_ROWS + HEAD_DIM, :] = vt[kv * HEAD_DIM:(kv + 1) * HEAD_DIM]
            av_ref[kv * V_ROWS + HEAD_DIM:(kv + 1) * V_ROWS, :] = ones_pad

    def gate_block():
        g = _dot(h, wt_ref[:, col_g:col_g + GATE_WIDTH])
        glane = lax.broadcasted_iota(jnp.int32, g.shape, 1) % HEAD_DIM
        t = g + dtb_ref[...]
        softplus = jnp.maximum(t, 0.0) + jnp.log1p(jnp.exp(-jnp.abs(t)))
        glog = -jnp.exp(alog_ref[...]) * softplus
        beta = jax.nn.sigmoid(g)
        gate_ref[...] = jnp.where(glane < B_HEADS, glog, jnp.where(glane < 2 * B_HEADS, beta, 0.0))

    def mq_block():
        mq_ref[...] = (_dot(h, wt_ref[:, col_m:col_m + M_WIDTH]) * (HEAD_DIM ** -0.5)).astype(BF16)

    def z_block(c0):
        z_ref[:, c0:c0 + 512] = _dot(h, wt_ref[:, col_z + c0:col_z + c0 + 512]).astype(BF16)

    for c0 in range(0, QK_WIDTH, 2 * HEAD_DIM):
        qk_pair(c0)
    v_block()
    for c0 in range(0, B_QKV_WIDTH, bw):
        b_proj(c0)
    for hh in range(B_QKV_WIDTH // HEAD_DIM):
        conv_head(hh)
    gate_block()
    mq_block()
    for c0 in range(0, D_MIX, 512):
        z_block(c0)


def _in_proj(x2, nw, w_bf, w_tail, qkw, cos_t, sin_t, cw, alog_v, dtb_v, seq, tm):
    rows = x2.shape[0]
    n_seq_tiles = seq // tm
    row = lambda i: (i, 0)
    const = lambda i: (0, 0)
    nb = rows // seq
    halo = BF16_ROWS
    per = tm // halo
    last_blk = rows // halo - 1
    outs = [
        (A_KV_WIDTH, BF16), (B_QKV_WIDTH, BF16), (GATE_WIDTH, F32), (M_WIDTH, BF16), (D_MIX, BF16),
    ]
    t_specs = [
        pl.BlockSpec((None, None, A_WIDTH, tm), lambda i: (i // n_seq_tiles, i % n_seq_tiles, 0, 0)),
        pl.BlockSpec((None, None, A_KV_HEADS * V_ROWS, tm), lambda i: (i // n_seq_tiles, i % n_seq_tiles, 0, 0)),
    ]
    t_shapes = [
        jax.ShapeDtypeStruct((nb, n_seq_tiles, A_WIDTH, tm), BF16),
        jax.ShapeDtypeStruct((nb, n_seq_tiles, A_KV_HEADS * V_ROWS, tm), BF16),
    ]
    return pl.pallas_call(
        functools.partial(_in_proj_body, tm=tm, n_seq_tiles=n_seq_tiles),
        grid=(rows // tm,),
        in_specs=[
            pl.BlockSpec((tm, D_MODEL), row),
            pl.BlockSpec((halo, D_MODEL), lambda i: (jnp.maximum(i * per - 1, 0), 0)),
            pl.BlockSpec((halo, D_MODEL), lambda i: (jnp.minimum((i + 1) * per, last_blk), 0)),
            pl.BlockSpec((1, D_MODEL), const),
            pl.BlockSpec((D_MODEL, HEAD_COLS), const, pipeline_mode=pl.Buffered(1)),
            pl.BlockSpec((D_MODEL, w_tail.shape[1]), const, pipeline_mode=pl.Buffered(1)),
            pl.BlockSpec((SUBLANES, HEAD_DIM), const),
            pl.BlockSpec((tm, HEAD_DIM), lambda i: (i % n_seq_tiles, 0)),
            pl.BlockSpec((tm, HEAD_DIM), lambda i: (i % n_seq_tiles, 0)),
            pl.BlockSpec((SUBLANES, B_QKV_WIDTH), const),
            pl.BlockSpec((1, GATE_WIDTH), const),
            pl.BlockSpec((1, GATE_WIDTH), const),
        ],
        out_specs=[t_specs[0], pl.BlockSpec((tm, outs[0][0]), row), t_specs[1]]
        + [pl.BlockSpec((tm, w), row) for w, _ in outs[1:]],
        out_shape=[t_shapes[0], jax.ShapeDtypeStruct((rows, outs[0][0]), outs[0][1]), t_shapes[1]]
        + [jax.ShapeDtypeStruct((rows, w), dt) for w, dt in outs[1:]],
        scratch_shapes=[pltpu.VMEM((tm + 2 * halo, B_QKV_WIDTH), F32)],
        compiler_params=_params(("parallel",)),
        name="in_proj",
    )(x2, x2, x2, nw, w_bf, w_tail, qkw, cos_t, sin_t, cw, alog_v, dtb_v)


def _attn_body(qt_ref, k_ref, vt_ref, z_ref, o_ref, m_ref, acc_ref, s_ref, p_ref, *, n_kv, n_q, tq):
    def scores(qi, j, slot, g):
        s_ref[slot, g] = _dot(k_ref[j], qt_ref[qi, g * HEAD_DIM:(g + 1) * HEAD_DIM, :])

    tk = s_ref.shape[2]
    sub = SUBLANES
    rc = 4 * sub

    def softmax_pv(j, slot, g):
        mx = s_ref[slot, g, 0:sub, :]
        for c in range(1, tk // sub):
            mx = jnp.maximum(mx, s_ref[slot, g, c * sub:(c + 1) * sub, :])
        m_new = jnp.max(mx, axis=0, keepdims=True)
        if j > 0:
            m_old = m_ref[g]
            m_new = jnp.maximum(m_old, m_new)
            alpha = jnp.exp2(m_old - m_new)
        m_ref[g] = m_new
        for c in range(tk // rc):
            pc = jnp.exp2(s_ref[slot, g, c * rc:(c + 1) * rc, :] - m_new)
            p_ref[slot, g, c * rc:(c + 1) * rc, :] = pc.astype(BF16)
        pv = _dot(vt_ref[j], p_ref[slot, g])
        if j == 0:
            acc_ref[g] = pv
        else:
            acc_ref[g] = alpha * acc_ref[g] + pv

    for g in range(A_GROUP):
        scores(0, 0, 0, g)

    def q_tile(qi, carry):
        for j in range(n_kv):
            slot = j % 2
            for g in range(A_GROUP):
                if j + 1 < n_kv:
                    scores(qi, j + 1, 1 - slot, g)
                else:
                    scores(jnp.minimum(qi + 1, n_q - 1), 0, 1 - slot, g)
                softmax_pv(j, slot, g)
        rows = pl.ds(pl.multiple_of(qi * tq, tq), tq)
        for g in range(A_GROUP):
            hs = slice(g * HEAD_DIM, (g + 1) * HEAD_DIM)
            y = (acc_ref[g, 0:HEAD_DIM, :] / acc_ref[g, HEAD_DIM:HEAD_DIM + 1, :]).T
            o_ref[rows, hs] = (y * _silu(z_ref[rows, hs].astype(F32))).astype(BF16)
        return carry

    lax.fori_loop(0, n_q, q_tile, 0)


def _attention(aqt, ak, avt, z):
    b, n_q, _, tq = aqt.shape
    n_kv, tk = avt.shape[1], avt.shape[3]
    seq = n_kv * tk
    assert n_kv % 2 == 0
    gw = A_GROUP * HEAD_DIM
    akr = ak.reshape(b, n_kv, tk, A_KV_WIDTH)
    return pl.pallas_call(
        functools.partial(_attn_body, n_kv=n_kv, n_q=n_q, tq=tq),
        grid=(b, A_KV_HEADS),
        in_specs=[
            pl.BlockSpec((None, n_q, gw, tq), lambda bi, kv: (bi, 0, kv, 0)),
            pl.BlockSpec((None, n_kv, tk, HEAD_DIM), lambda bi, kv: (bi, 0, 0, kv)),
            pl.BlockSpec((None, n_kv, V_ROWS, tk), lambda bi, kv: (bi, 0, kv, 0)),
            pl.BlockSpec((None, seq, gw), lambda bi, kv: (bi, 0, kv)),
        ],
        out_specs=pl.BlockSpec((None, seq, gw), lambda bi, kv: (bi, 0, kv)),
        out_shape=jax.ShapeDtypeStruct((b, seq, A_WIDTH), BF16),
        scratch_shapes=[pltpu.VMEM((A_GROUP, 1, tq), F32), pltpu.VMEM((A_GROUP, V_ROWS, tq), F32), pltpu.VMEM((2, A_GROUP, tk, tq), F32),
                        pltpu.VMEM((2, A_GROUP, tk, tq), BF16)],
        compiler_params=_params(("parallel", "arbitrary")),
        name="attn",
    )(aqt, akr, avt, z)


def _split2(a):
    hi = a.astype(BF16)
    lo = (a - hi.astype(F32)).astype(BF16)
    return [hi, lo]


def _lane_blocks(cols, width, block):
    rows = cols[0].shape[0]
    blk = lax.broadcasted_iota(jnp.int32, (rows, width), 1) // block
    out = jnp.broadcast_to(cols[-1], (rows, width))
    for i in range(len(cols) - 2, -1, -1):
        out = jnp.where(blk == i, cols[i], out)
    return out


def _tile_rows(x, n):
    return jnp.concatenate([x] * n, axis=0)


def _gdn_body(qf_ref, kf_ref, vf_ref, gf_ref, qb_ref, kb_ref, vb_ref, gb_ref, of_ref, ob_ref,
              s_ref, mq_ref, bo_ref, dc_ref, *, gsz, nb):
    c_, h_, w_ = CHUNK, B_HEADS, B_HEADS * CHUNK
    n_pair = h_ // 2
    pw = 2 * HEAD_DIM
    step = pl.program_id(0)

    @pl.when(step == 0)
    def _():
        s_ref[...] = jnp.zeros_like(s_ref)
        mq_ref[...] = jnp.zeros_like(mq_ref)
        bo_ref[...] = jnp.zeros_like(bo_ref)
        dc_ref[...] = jnp.zeros_like(dc_ref)

    keep = jnp.where((step - 1) % nb == 0, 0.0, 1.0)
    o_refs = (of_ref, ob_ref)
    zero_blk = jnp.zeros((HEAD_DIM, HEAD_DIM), BF16)
    for g in range(gsz):
        for d in range(N_DIR):
            rows = g if d == 0 else gsz - 1 - g
            for pr in range(n_pair):
                state = s_ref[d * n_pair + pr]
                if g == 0:
                    state = state * keep
                sb = state.astype(BF16)
                s_bd = jnp.concatenate([jnp.concatenate([sb[:, :HEAD_DIM], zero_blk], axis=1),
                                        jnp.concatenate([zero_blk, sb[:, HEAD_DIM:]], axis=1)], axis=0)
                r = _dot(mq_ref[g, d, pr], s_bd)
                bo = bo_ref[g, d, pr]
                s_ref[d * n_pair + pr] = (state * dc_ref[g, d, 0:1, pr * pw:(pr + 1) * pw]
                                          + r[:HEAD_DIM] + bo[:HEAD_DIM])
                o_refs[d][rows * c_:(rows + 1) * c_, pr * pw:(pr + 1) * pw] = (
                    r[HEAD_DIM:] + bo[HEAD_DIM:]).astype(BF16)

    ri = lax.broadcasted_iota(jnp.int32, (c_, w_), 0)
    jj = lax.broadcasted_iota(jnp.int32, (c_, w_), 1) % c_
    eye_w = (ri == jj).astype(F32)
    r2 = lax.broadcasted_iota(jnp.int32, (c_, c_), 0)
    c2 = lax.broadcasted_iota(jnp.int32, (c_, c_), 1)

    def blockdiag_mask(width, lanes_per_head, period=None):
        rh = lax.broadcasted_iota(jnp.int32, (w_, width), 0) // c_
        ln = lax.broadcasted_iota(jnp.int32, (w_, width), 1)
        if period is not None:
            ln = ln % period
        return (rh == ln // lanes_per_head).astype(BF16)

    bd_mask = blockdiag_mask(w_, c_)
    bdk_mask = blockdiag_mask(B_WIDTH, HEAD_DIM)
    bdr_mask = blockdiag_mask(2 * B_WIDTH, HEAD_DIM, period=B_WIDTH)

    dirs = []
    for d in range(N_DIR):
        diff = (ri - jj) if d == 0 else (jj - ri)
        d2 = (r2 - c2) if d == 0 else (c2 - r2)
        dirs.append(dict(incl=diff >= 0, strict=diff > 0, m2=(diff > 0).astype(F32),
                         m1=(d2 >= 0).astype(BF16)))
    row8 = lax.broadcasted_iota(jnp.int32, (SUBLANES, HEAD_DIM), 0)
    srcs = ((qf_ref, kf_ref, vf_ref, gf_ref), (qb_ref, kb_ref, vb_ref, gb_ref))

    def st_scores(t):
        q_ref, k_ref, v_ref, g_ref = srcs[t["d"]]
        pos = t["g"] if t["d"] == 0 else gsz - 1 - t["g"]
        sl = slice(pos * c_, (pos + 1) * c_)
        kc = k_ref[sl, :]
        kq = _dot(jnp.concatenate([kc, q_ref[sl, :]], axis=0), _tile_rows(kc, h_) * bdk_mask, _NT)
        t.update(m=dirs[t["d"]], sl=sl, q_ref=q_ref, k_ref=k_ref, v_ref=v_ref, g_ref=g_ref,
                 kk_w=kq[:c_], qk_w=kq[c_:])

    def st_decay(t):
        m = t["m"]
        gt = t["g_ref"][t["sl"], :]
        gl_w = _lane_blocks([gt[:, h:h + 1] for h in range(h_)], w_, c_)
        d3 = _dot(m["m1"], jnp.concatenate(_split2(gl_w * m["m2"]) + _split2(gt), axis=1))
        dm_w = d3[:, :w_] + d3[:, w_:2 * w_]
        o3 = 2 * w_
        t["gcum"] = d3[:, o3:o3 + HEAD_DIM] + d3[:, o3 + HEAD_DIM:]
        t["gt"] = gt
        t["decay"] = jnp.exp(dm_w)
        beta_w = _lane_blocks([gt[:, h_ + h:h_ + h + 1] for h in range(h_)], w_, c_)
        t["x"] = -jnp.where(m["strict"], t["kk_w"] * beta_w * t["decay"], 0.0)

    def st_inv_first(t):
        xb = t["x"].astype(BF16)
        t["xp"] = _dot(xb, _tile_rows(xb, h_) * bd_mask)
        t["p"] = eye_w + t["x"]

    def st_inv_level(t):
        xpb = t["xp"].astype(BF16)
        r = _dot(jnp.concatenate([t["p"].astype(BF16), xpb], axis=0), _tile_rows(xpb, h_) * bd_mask)
        t["p"] = t["p"] + r[:c_]
        t["xp"] = r[c_:]

    def st_inv_last(t):
        t["p"] = t["p"] + _dot(t["p"].astype(BF16), _tile_rows(t["xp"].astype(BF16), h_) * bd_mask)

    def st_uw(t):
        gt, sl = t["gt"], t["sl"]
        kf = t["k_ref"][sl, :].astype(F32)
        vf = t["v_ref"][sl, :].astype(F32)
        g_hd = _lane_blocks([t["gcum"][:, h:h + 1] for h in range(h_)], B_WIDTH, HEAD_DIM)
        beta_hd = _lane_blocks([gt[:, h_ + h:h_ + h + 1] for h in range(h_)], B_WIDTH, HEAD_DIM)
        gtot = jnp.sum(gt, axis=0, keepdims=True)
        gtot_hd = _lane_blocks([gtot[:, h:h + 1] for h in range(h_)], B_WIDTH, HEAD_DIM)
        t["eg_hd"] = jnp.exp(g_hd)
        t["dc_hd"] = jnp.exp(gtot_hd)
        t["k_tail"] = (kf * jnp.exp(gtot_hd - g_hd)).astype(BF16)
        rhs = jnp.concatenate([vf * beta_hd, kf * (beta_hd * t["eg_hd"])], axis=1).astype(BF16)
        t["uw"] = _dot(t["p"].astype(BF16), _tile_rows(rhs, h_) * bdr_mask)

    def st_intra(t):
        intra = jnp.where(t["m"]["incl"], t["qk_w"] * t["decay"], 0.0).astype(BF16)
        t["iuw"] = _dot(intra, _tile_rows(t["uw"].astype(BF16), h_) * bdr_mask)

    def st_mb(t):
        uw = t["uw"]
        t["mb"] = []
        for h in range(h_):
            hs = slice(h * HEAD_DIM, (h + 1) * HEAD_DIM)
            wu = jnp.concatenate([-uw[:, B_WIDTH + h * HEAD_DIM:B_WIDTH + (h + 1) * HEAD_DIM], uw[:, hs]],
                                 axis=1).astype(BF16)
            t["mb"].append(_dot(t["k_tail"][:, hs], wu, _TN))

    def st_store(t):
        g, d = t["g"], t["d"]
        qf = t["q_ref"][t["sl"], :].astype(F32)
        o_local = t["iuw"][:, :B_WIDTH]
        q_eff = (qf * t["eg_hd"] - t["iuw"][:, B_WIDTH:]).astype(BF16)
        for h in range(h_):
            pr, ls = h // 2, slice((h % 2) * HEAD_DIM, (h % 2 + 1) * HEAD_DIM)
            mb = t["mb"][h]
            mq_ref[g, d, pr, 0:HEAD_DIM, ls] = mb[:, :HEAD_DIM].astype(BF16)
            bo_ref[g, d, pr, 0:HEAD_DIM, ls] = mb[:, HEAD_DIM:]
        for pr in range(n_pair):
            ps = slice(pr * pw, (pr + 1) * pw)
            mq_ref[g, d, pr, HEAD_DIM:, :] = q_eff[:, ps]
            bo_ref[g, d, pr, HEAD_DIM:, :] = o_local[:, ps]
        dc_ref[g, d] = jnp.broadcast_to(t["dc_hd"], (SUBLANES, B_WIDTH))

    stages = ([st_scores, st_decay, st_inv_first] + [st_inv_level] * 4
              + [st_inv_last, st_uw, st_intra, st_mb, st_store])
    inst = [dict(g=g, d=d) for g in range(gsz) for d in range(N_DIR)]
    half = len(inst) // 2
    groups = (inst[:half], inst[half:])
    for k in range(len(stages) + GDN_SKEW):
        for idx in range(half):
            if k < len(stages):
                stages[k](groups[0][idx])
            if 0 <= k - GDN_SKEW < len(stages):
                stages[k - GDN_SKEW](groups[1][idx])


def _gdn(qkvn, gates, gsz):
    b, seq, _ = qkvn.shape
    nb = seq // (gsz * CHUNK)
    rows = gsz * CHUNK
    mrows = HEAD_DIM + CHUNK

    total = b * nb
    n_pair = B_HEADS // 2

    def cur(t):
        return jnp.minimum(t, total - 1)

    def prev(t):
        return jnp.maximum(t - 1, 0)

    def fwd_in(j):
        return lambda t: (cur(t) // nb, cur(t) % nb, j)

    def bwd_in(j):
        return lambda t: (cur(t) // nb, nb - 1 - cur(t) % nb, j)

    out = jax.ShapeDtypeStruct((b, seq, B_WIDTH), BF16)
    return pl.pallas_call(
        functools.partial(_gdn_body, gsz=gsz, nb=nb),
        grid=(total + 1,),
        in_specs=[pl.BlockSpec((None, rows, B_WIDTH), fwd_in(j)) for j in range(3)]
        + [pl.BlockSpec((None, rows, HEAD_DIM), fwd_in(0))]
        + [pl.BlockSpec((None, rows, B_WIDTH), bwd_in(j)) for j in range(3)]
        + [pl.BlockSpec((None, rows, HEAD_DIM), bwd_in(1))],
        out_specs=[
            pl.BlockSpec((None, rows, B_WIDTH), lambda t: (prev(t) // nb, prev(t) % nb, 0)),
            pl.BlockSpec((None, rows, B_WIDTH), lambda t: (prev(t) // nb, nb - 1 - prev(t) % nb, 0)),
        ],
        out_shape=[out, out],
        scratch_shapes=[
            pltpu.VMEM((N_DIR * n_pair, HEAD_DIM, 2 * HEAD_DIM), F32),
            pltpu.VMEM((gsz, N_DIR, n_pair, mrows, 2 * HEAD_DIM), BF16),
            pltpu.VMEM((gsz, N_DIR, n_pair, mrows, 2 * HEAD_DIM), F32),
            pltpu.VMEM((gsz, N_DIR, SUBLANES, B_WIDTH), F32),
        ],
        compiler_params=_params(("arbitrary",)),
        name="gdn",
    )(qkvn, qkvn, qkvn, gates, qkvn, qkvn, qkvn, gates)


def _mem_kv_body(m_ref, nw_ref, w_ref, o_ref):
    x = m_ref[...]
    ms = jnp.mean(x * x, axis=-1, keepdims=True)
    h = (x * lax.rsqrt(ms + EPS) * nw_ref[...]).astype(BF16)
    o_ref[...] = _dot(h, w_ref[...]).astype(BF16)


def _mem_kv(mem2, nw, w, tm):
    rows = mem2.shape[0]
    return pl.pallas_call(
        _mem_kv_body,
        grid=(rows // tm,),
        in_specs=[
            pl.BlockSpec((tm, D_MODEL), lambda i: (i, 0)),
            pl.BlockSpec((1, D_MODEL), lambda i: (0, 0)),
            pl.BlockSpec((D_MODEL, 2 * M_WIDTH), lambda i: (0, 0)),
        ],
        out_specs=pl.BlockSpec((tm, 2 * M_WIDTH), lambda i: (i, 0)),
        out_shape=jax.ShapeDtypeStruct((rows, 2 * M_WIDTH), BF16),
        compiler_params=_params(("parallel",)),
        name="mem_kv",
    )(mem2, nw, w)


def _out_proj_body(ya_ref, of_ref, ob_ref, z_ref, mq_ref, kv_ref, dnw_ref, w_ref, pw_ref, x_ref, o_ref):
    acc = _dot(ya_ref[...], w_ref[0:A_WIDTH, :])
    ob = of_ref[...].astype(F32) + ob_ref[...].astype(F32)
    dnw = dnw_ref[...]
    parts = []
    for h in range(B_HEADS):
        hs = slice(h * HEAD_DIM, (h + 1) * HEAD_DIM)
        oh = ob[:, hs]
        yn = oh * lax.rsqrt(jnp.mean(oh * oh, axis=-1, keepdims=True) + EPS) * dnw
        parts.append((yn * _silu(z_ref[:, hs].astype(F32))).astype(BF16))
    yb = jnp.concatenate(parts, axis=1)
    parts = []
    for h in range(M_HEADS):
        hs = slice(h * HEAD_DIM, (h + 1) * HEAD_DIM)
        s = _dot(mq_ref[:, hs], kv_ref[:, hs], _NT)
        p = jnp.exp(s - jnp.max(s, axis=-1, keepdims=True))
        l = jnp.sum(p, axis=-1, keepdims=True)
        y = _dot(p.astype(BF16), kv_ref[:, M_WIDTH + h * HEAD_DIM:M_WIDTH + (h + 1) * HEAD_DIM]) / l
        zs = slice(B_WIDTH + h * HEAD_DIM, B_WIDTH + (h + 1) * HEAD_DIM)
        parts.append((y * _silu(z_ref[:, zs].astype(F32))).astype(BF16))
    ym = jnp.concatenate(parts, axis=1)
    acc = acc + _dot(yb, w_ref[A_WIDTH:A_WIDTH + B_WIDTH, :])
    acc = acc + _dot(ym, w_ref[A_WIDTH + B_WIDTH:, :])
    y = acc * lax.rsqrt(jnp.mean(acc * acc, axis=-1, keepdims=True) + EPS) * pw_ref[...]
    o_ref[...] = x_ref[...] + y


def _out_proj(ya, o_f, o_b, z, mq, mkv, dnw, w, pw, x, tm):
    b, seq, _ = ya.shape
    n_mem = mkv.shape[1]
    zw = B_WIDTH + M_WIDTH
    assert A_WIDTH % zw == 0
    row = lambda bi, i: (bi, i, 0)
    const = lambda bi, i: (0, 0)
    return pl.pallas_call(
        _out_proj_body,
        grid=(b, seq // tm),
        in_specs=[
            pl.BlockSpec((None, tm, A_WIDTH), row),
            pl.BlockSpec((None, tm, B_WIDTH), row),
            pl.BlockSpec((None, tm, B_WIDTH), row),
            pl.BlockSpec((None, tm, zw), lambda bi, i: (bi, i, A_WIDTH // zw)),
            pl.BlockSpec((None, tm, M_WIDTH), row),
            pl.BlockSpec((None, n_mem, 2 * M_WIDTH), lambda bi, i: (bi, 0, 0)),
            pl.BlockSpec((1, HEAD_DIM), const),
            pl.BlockSpec((D_MIX, D_MODEL), const),
            pl.BlockSpec((1, D_MODEL), const),
            pl.BlockSpec((None, tm, D_MODEL), row),
        ],
        out_specs=pl.BlockSpec((None, tm, D_MODEL), row),
        out_shape=jax.ShapeDtypeStruct((b, seq, D_MODEL), F32),
        compiler_params=_params(("parallel", "parallel")),
        name="out_proj",
    )(ya, o_f, o_b, z, mq, mkv, dnw, w, pw, x)


def _rope_tables(seq):
    t = np.arange(seq)
    inv_freq = (np.float32(ROPE_THETA) ** (-np.arange(0, ROPE_AXIS_DIM, 2, dtype=np.float32) / ROPE_AXIS_DIM)
                ).astype(np.float32)
    ang_row = (t // GRID_W).astype(np.float32)[:, None] * inv_freq
    ang_col = (t % GRID_W).astype(np.float32)[:, None] * inv_freq
    cr, sr, cc, sc = np.cos(ang_row), np.sin(ang_row), np.cos(ang_col), np.sin(ang_col)
    return (jnp.asarray(np.concatenate([cr, cr, cc, cc], axis=-1), F32),
            jnp.asarray(np.concatenate([-sr, sr, -sc, sc], axis=-1), F32))


def _arrange_w_tail(w):
    pts = np.cumsum([N_DIR * B_HEADS, N_DIR * B_HEADS, M_WIDTH])
    ba, bb, mq, z = jnp.split(w[:, HEAD_COLS:], pts, axis=1)
    gate_blocks = []
    for d in range(N_DIR):
        sl = slice(d * B_HEADS, (d + 1) * B_HEADS)
        pad = jnp.zeros((w.shape[0], HEAD_DIM - 2 * B_HEADS), w.dtype)
        gate_blocks += [ba[:, sl], bb[:, sl], pad]
    return jnp.concatenate(gate_blocks + [mq, z], axis=1)


def _gate_lane_vec(p):
    out = jnp.zeros((N_DIR, HEAD_DIM), F32).at[:, :B_HEADS].set(p.astype(F32))
    return out.reshape(1, GATE_WIDTH)


def kernel(x, mem, norm_pre_w, w_in, q_norm_w, k_norm_w, conv_w, a_log, dt_bias, delta_norm_w,
           mem_norm_w, w_mem_kv, w_out, norm_post_w):
    b, seq, _ = x.shape
    n_mem = mem.shape[1]
    assert seq % 512 == 0 and seq % GRID_W == 0 and w_in.shape[0] == 1
    l = 0
    tm = 512

    cos_t, sin_t = _rope_tables(seq)
    qkw = jnp.zeros((SUBLANES, HEAD_DIM), F32).at[0].set(q_norm_w[l]).at[1].set(k_norm_w[l])
    cw = jnp.zeros((SUBLANES, B_QKV_WIDTH), F32).at[:CONV_K].set(conv_w[l])
    w_bf = w_in[l].astype(BF16)
    aq, ak, av, qkvn, gates, mq, z = _in_proj(
        x.reshape(b * seq, D_MODEL), norm_pre_w[l].reshape(1, D_MODEL), w_bf, _arrange_w_tail(w_bf),
        qkw, cos_t, sin_t, cw, _gate_lane_vec(a_log[l]), _gate_lane_vec(dt_bias[l]), seq, tm)
    shp = lambda a: a.reshape(b, seq, a.shape[-1])
    ak, qkvn, gates, mq, z = map(shp, (ak, qkvn, gates, mq, z))

    ya = _attention(aq, ak, av, z)
    o_f, o_b = _gdn(qkvn, gates, gsz=8)

    mkv = _mem_kv(mem.reshape(b * n_mem, D_MODEL), mem_norm_w[l].reshape(1, D_MODEL),
                  w_mem_kv[l].astype(BF16), tm=256)

    return _out_proj(ya, o_f, o_b, z, mq, mkv.reshape(b, n_mem, 2 * M_WIDTH),
                     delta_norm_w[l].reshape(1, HEAD_DIM), w_out[l].astype(BF16),
                     norm_post_w[l].reshape(1, D_MODEL), x, tm=512)
```

```python
import functools

import jax
import jax.numpy as jnp
import numpy as np
from jax import lax
from jax.experimental import pallas as pl
from jax.experimental.pallas import tpu as pltpu

F32 = jnp.float32
BF16 = jnp.bfloat16

D_MODEL = 1024
HEAD_DIM = 128
GRID_W = 64
A_HEADS = 8
A_KV_HEADS = 2
A_GROUP = A_HEADS // A_KV_HEADS
A_WIDTH = A_HEADS * HEAD_DIM
A_KV_WIDTH = A_KV_HEADS * HEAD_DIM
B_HEADS = 4
B_WIDTH = B_HEADS * HEAD_DIM
B_QKV_WIDTH = 3 * B_WIDTH
N_DIR = 2
CONV_K = 5
CHUNK = 64
M_HEADS = 4
M_WIDTH = M_HEADS * HEAD_DIM
D_MIX = A_WIDTH + B_WIDTH + M_WIDTH
ROPE_AXIS_DIM = HEAD_DIM // 2
ROPE_THETA = 10000.0
EPS = 1e-6
LOG2E = 1.4426950408889634
GDN_SKEW = 1
SEQ_TILE = 512
GDN_CHUNKS_PER_STEP = 8
MEM_ROW_TILE = 256
GATE_WIDTH = N_DIR * HEAD_DIM
QK_WIDTH = A_WIDTH + A_KV_WIDTH
HEAD_COLS = QK_WIDTH + A_KV_WIDTH + B_QKV_WIDTH
SUBLANES = 8
BF16_ROWS = 2 * SUBLANES
V_ROWS = HEAD_DIM + BF16_ROWS
VMEM_LIMIT = 56 * 1024 * 1024

_NT = (((1,), (1,)), ((), ()))
_TN = (((0,), (0,)), ((), ()))


def _dot(a, b, dims=None, precision=None):
    if dims is None:
        return jnp.dot(a, b, preferred_element_type=F32, precision=precision)
    return lax.dot_general(a, b, dims, preferred_element_type=F32, precision=precision)


def _silu(x):
    return x * jax.nn.sigmoid(x)


def _params(sem):
    return pltpu.CompilerParams(dimension_semantics=sem, vmem_limit_bytes=VMEM_LIMIT)


def _in_proj_body(x_ref, xp_ref, xn_ref, nw_ref, w_ref, wt_ref, qkw_ref, cos_ref, sin_ref, cw_ref, alog_ref,
                  dtb_ref, aq_ref, ak_ref, av_ref, qkv_ref, gate_ref, mq_ref, z_ref, ext_ref, *, tm, n_seq_tiles):
    def pre_norm(x):
        ms = jnp.mean(x * x, axis=-1, keepdims=True)
        return (x * lax.rsqrt(ms + EPS) * nw_ref[...]).astype(BF16)

    h = pre_norm(x_ref[...])

    cos = cos_ref[...]
    sin = sin_ref[...]
    lane = lax.broadcasted_iota(jnp.int32, cos.shape, 1)
    first_half = (lane % (ROPE_AXIS_DIM)) < (ROPE_AXIS_DIM // 2)

    def norm_rope(y, w):
        yn = y * lax.rsqrt(jnp.mean(y * y, axis=-1, keepdims=True) + EPS) * w
        swapped = jnp.where(first_half,
                            pltpu.roll(yn, HEAD_DIM - ROPE_AXIS_DIM // 2, 1),
                            pltpu.roll(yn, ROPE_AXIS_DIM // 2, 1))
        return yn * cos + swapped * sin

    col_v = QK_WIDTH
    col_b = col_v + A_KV_WIDTH
    col_g = 0
    col_m = col_g + GATE_WIDTH
    col_z = col_m + M_WIDTH

    ti = pl.program_id(0) % n_seq_tiles
    halo = BF16_ROWS
    h_ext = jnp.concatenate([pre_norm(xp_ref[...]), h, pre_norm(xn_ref[...])], axis=0)
    bw = 2 * HEAD_DIM

    def b_proj(c0):
        r = _dot(h_ext, w_ref[:, col_b + c0:col_b + c0 + bw])
        ext_ref[0:halo, c0:c0 + bw] = jnp.where(ti == 0, 0.0, r[0:halo])
        ext_ref[halo:halo + tm, c0:c0 + bw] = r[halo:halo + tm]
        ext_ref[halo + tm:, c0:c0 + bw] = jnp.where(ti == n_seq_tiles - 1, 0.0, r[halo + tm:])

    def conv_head(hh):
        cs = slice(hh * HEAD_DIM, (hh + 1) * HEAD_DIM)
        xe = ext_ref[:, cs]
        n_ext = xe.shape[0]
        y = None
        for j in range(CONV_K):
            sh = (CONV_K // 2 - j) % n_ext
            xs = xe if sh == 0 else pltpu.roll(xe, sh, 0)
            term = xs[halo:halo + tm] * cw_ref[j:j + 1, cs]
            y = term if y is None else y + term
        y = _silu(y)
        if hh < 2 * B_HEADS:
            y = y * lax.rsqrt(jnp.sum(y * y, axis=-1, keepdims=True) + EPS)
        if hh < B_HEADS:
            y = y * (HEAD_DIM ** -0.5)
        qkv_ref[:, cs] = y.astype(BF16)

    def qk_pair(c0):
        y = _dot(h, w_ref[:, c0:c0 + 2 * HEAD_DIM])
        for j in range(2):
            hc = c0 + j * HEAD_DIM
            yh = y[:, j * HEAD_DIM:(j + 1) * HEAD_DIM]
            if hc < A_WIDTH:
                out = norm_rope(yh, qkw_ref[0:1, :]) * (HEAD_DIM ** -0.5 * LOG2E)
                aq_ref[hc:hc + HEAD_DIM, :] = out.T.astype(BF16)
            else:
                out = norm_rope(yh, qkw_ref[1:2, :])
                ak_ref[:, hc - A_WIDTH:hc - A_WIDTH + HEAD_DIM] = out.astype(BF16)

    def v_block():
        vt = _dot(h, w_ref[:, col_v:col_v + A_KV_WIDTH]).T.astype(BF16)
        pad_row = lax.broadcasted_iota(jnp.int32, (V_ROWS - HEAD_DIM, vt.shape[1]), 0)
        ones_pad = jnp.where(pad_row == 0, 1.0, 0.0).astype(BF16)
        for kv in range(A_KV_HEADS):
            av_ref[kv * V_ROWS:kv * V_ROWS + HEAD_DIM, :] = vt[kv * HEAD_DIM:(kv + 1) * HEAD_DIM]
            av_ref[kv * V_ROWS + HEAD_DIM:(kv + 1) * V_ROWS, :] = ones_pad

    def gate_block():
        g = _dot(h, wt_ref[:, col_g:col_g + GATE_WIDTH])
        glane = lax.broadcasted_iota(jnp.int32, g.shape, 1) % HEAD_DIM
        t = g + dtb_ref[...]
        softplus = jnp.maximum(t, 0.0) + jnp.log1p(jnp.exp(-jnp.abs(t)))
        glog = -jnp.exp(alog_ref[...]) * softplus
        beta = jax.nn.sigmoid(g)
        gate_ref[...] = jnp.where(glane < B_HEADS, glog, jnp.where(glane < 2 * B_HEADS, beta, 0.0))

    def mq_block():
        mq_ref[...] = (_dot(h, wt_ref[:, col_m:col_m + M_WIDTH]) * (HEAD_DIM ** -0.5)).astype(BF16)

    def z_block(c0):
        z_ref[:, c0:c0 + 512] = _dot(h, wt_ref[:, col_z + c0:col_z + c0 + 512]).astype(BF16)

    for c0 in range(0, QK_WIDTH, 2 * HEAD_DIM):
        qk_pair(c0)
    v_block()
    for c0 in range(0, B_QKV_WIDTH, bw):
        b_proj(c0)
    for hh in range(B_QKV_WIDTH // HEAD_DIM):
        conv_head(hh)
    gate_block()
    mq_block()
    for c0 in range(0, D_MIX, 512):
        z_block(c0)


def _in_proj(x2, nw, w_bf, w_tail, qkw, cos_t, sin_t, cw, alog_v, dtb_v, seq, tm):
    rows = x2.shape[0]
    n_seq_tiles = seq // tm
    row = lambda i: (i, 0)
    const = lambda i: (0, 0)
    nb = rows // seq
    halo = BF16_ROWS
    per = tm // halo
    last_blk = rows // halo - 1
    outs = [
        (A_KV_WIDTH, BF16), (B_QKV_WIDTH, BF16), (GATE_WIDTH, F32), (M_WIDTH, BF16), (D_MIX, BF16),
    ]
    t_specs = [
        pl.BlockSpec((None, None, A_WIDTH, tm), lambda i: (i // n_seq_tiles, i % n_seq_tiles, 0, 0)),
        pl.BlockSpec((None, None, A_KV_HEADS * V_ROWS, tm), lambda i: (i // n_seq_tiles, i % n_seq_tiles, 0, 0)),
    ]
    t_shapes = [
        jax.ShapeDtypeStruct((nb, n_seq_tiles, A_WIDTH, tm), BF16),
        jax.ShapeDtypeStruct((nb, n_seq_tiles, A_KV_HEADS * V_ROWS, tm), BF16),
    ]
    return pl.pallas_call(
        functools.partial(_in_proj_body, tm=tm, n_seq_tiles=n_seq_tiles),
        grid=(rows // tm,),
        in_specs=[
            pl.BlockSpec((tm, D_MODEL), row),
            pl.BlockSpec((halo, D_MODEL), lambda i: (jnp.maximum(i * per - 1, 0), 0)),
            pl.BlockSpec((halo, D_MODEL), lambda i: (jnp.minimum((i + 1) * per, last_blk), 0)),
            pl.BlockSpec((1, D_MODEL), const),
            pl.BlockSpec((D_MODEL, HEAD_COLS), const, pipeline_mode=pl.Buffered(1)),
            pl.BlockSpec((D_MODEL, w_tail.shape[1]), const, pipeline_mode=pl.Buffered(1)),
            pl.BlockSpec((SUBLANES, HEAD_DIM), const),
            pl.BlockSpec((tm, HEAD_DIM), lambda i: (i % n_seq_tiles, 0)),
            pl.BlockSpec((tm, HEAD_DIM), lambda i: (i % n_seq_tiles, 0)),
            pl.BlockSpec((SUBLANES, B_QKV_WIDTH), const),
            pl.BlockSpec((1, GATE_WIDTH), const),
            pl.BlockSpec((1, GATE_WIDTH), const),
        ],
        out_specs=[t_specs[0], pl.BlockSpec((tm, outs[0][0]), row), t_specs[1]]
        + [pl.BlockSpec((tm, w), row) for w, _ in outs[1:]],
        out_shape=[t_shapes[0], jax.ShapeDtypeStruct((rows, outs[0][0]), outs[0][1]), t_shapes[1]]
        + [jax.ShapeDtypeStruct((rows, w), dt) for w, dt in outs[1:]],
        scratch_shapes=[pltpu.VMEM((tm + 2 * halo, B_QKV_WIDTH), F32)],
        compiler_params=_params(("parallel",)),
        name="in_proj",
    )(x2, x2, x2, nw, w_bf, w_tail, qkw, cos_t, sin_t, cw, alog_v, dtb_v)


def _attn_body(qt_ref, k_ref, vt_ref, z_ref, o_ref, m_ref, acc_ref, s_ref, p_ref, *, n_kv, n_q, tq):
    def scores(qi, j, slot, g):
        s_ref[slot, g] = _dot(k_ref[j], qt_ref[qi, g * HEAD_DIM:(g + 1) * HEAD_DIM, :])

    tk = s_ref.shape[2]
    sub = SUBLANES
    rc = 4 * sub

    def softmax_pv(j, slot, g):
        mx = s_ref[slot, g, 0:sub, :]
        for c in range(1, tk // sub):
            mx = jnp.maximum(mx, s_ref[slot, g, c * sub:(c + 1) * sub, :])
        m_new = jnp.max(mx, axis=0, keepdims=True)
        if j > 0:
            m_old = m_ref[g]
            m_new = jnp.maximum(m_old, m_new)
            alpha = jnp.exp2(m_old - m_new)
        m_ref[g] = m_new
        for c in range(tk // rc):
            pc = jnp.exp2(s_ref[slot, g, c * rc:(c + 1) * rc, :] - m_new)
            p_ref[slot, g, c * rc:(c + 1) * rc, :] = pc.astype(BF16)
        pv = _dot(vt_ref[j], p_ref[slot, g])
        if j == 0:
            acc_ref[g] = pv
        else:
            acc_ref[g] = alpha * acc_ref[g] + pv

    for g in range(A_GROUP):
        scores(0, 0, 0, g)

    def q_tile(qi, carry):
        for j in range(n_kv):
            slot = j % 2
            for g in range(A_GROUP):
                if j + 1 < n_kv:
                    scores(qi, j + 1, 1 - slot, g)
                else:
                    scores(jnp.minimum(qi + 1, n_q - 1), 0, 1 - slot, g)
                softmax_pv(j, slot, g)
        rows = pl.ds(pl.multiple_of(qi * tq, tq), tq)
        for g in range(A_GROUP):
            hs = slice(g * HEAD_DIM, (g + 1) * HEAD_DIM)
            y = (acc_ref[g, 0:HEAD_DIM, :] / acc_ref[g, HEAD_DIM:HEAD_DIM + 1, :]).T
            o_ref[rows, hs] = (y * _silu(z_ref[rows, hs].astype(F32))).astype(BF16)
        return carry

    lax.fori_loop(0, n_q, q_tile, 0)


def _attention(aqt, ak, avt, z):
    b, n_q, _, tq = aqt.shape
    n_kv, tk = avt.shape[1], avt.shape[3]
    seq = n_kv * tk
    assert n_kv % 2 == 0
    gw = A_GROUP * HEAD_DIM
    akr = ak.reshape(b, n_kv, tk, A_KV_WIDTH)
    return pl.pallas_call(
        functools.partial(_attn_body, n_kv=n_kv, n_q=n_q, tq=tq),
        grid=(b, A_KV_HEADS),
        in_specs=[
            pl.BlockSpec((None, n_q, gw, tq), lambda bi, kv: (bi, 0, kv, 0)),
            pl.BlockSpec((None, n_kv, tk, HEAD_DIM), lambda bi, kv: (bi, 0, 0, kv)),
            pl.BlockSpec((None, n_kv, V_ROWS, tk), lambda bi, kv: (bi, 0, kv, 0)),
            pl.BlockSpec((None, seq, gw), lambda bi, kv: (bi, 0, kv)),
        ],
        out_specs=pl.BlockSpec((None, seq, gw), lambda bi, kv: (bi, 0, kv)),
        out_shape=jax.ShapeDtypeStruct((b, seq, A_WIDTH), BF16),
        scratch_shapes=[pltpu.VMEM((A_GROUP, 1, tq), F32), pltpu.VMEM((A_GROUP, V_ROWS, tq), F32), pltpu.VMEM((2, A_GROUP, tk, tq), F32),
                        pltpu.VMEM((2, A_GROUP, tk, tq), BF16)],
        compiler_params=_params(("parallel", "arbitrary")),
        name="attn",
    )(aqt, akr, avt, z)


def _split2(a):
    hi = a.astype(BF16)
    lo = (a - hi.astype(F32)).astype(BF16)
    return [hi, lo]


def _lane_blocks(cols, width, block):
    rows = cols[0].shape[0]
    blk = lax.broadcasted_iota(jnp.int32, (rows, width), 1) // block
    out = jnp.broadcast_to(cols[-1], (rows, width))
    for i in range(len(cols) - 2, -1, -1):
        out = jnp.where(blk == i, cols[i], out)
    return out


def _tile_rows(x, n):
    return jnp.concatenate([x] * n, axis=0)


def _gdn_body(qf_ref, kf_ref, vf_ref, gf_ref, qb_ref, kb_ref, vb_ref, gb_ref, of_ref, ob_ref,
              s_ref, mq_ref, bo_ref, dc_ref, *, gsz, nb):
    c_, h_, w_ = CHUNK, B_HEADS, B_HEADS * CHUNK
    n_pair = h_ // 2
    pw = 2 * HEAD_DIM
    step = pl.program_id(0)

    @pl.when(step == 0)
    def _():
        s_ref[...] = jnp.zeros_like(s_ref)
        mq_ref[...] = jnp.zeros_like(mq_ref)
        bo_ref[...] = jnp.zeros_like(bo_ref)
        dc_ref[...] = jnp.zeros_like(dc_ref)

    keep = jnp.where((step - 1) % nb == 0, 0.0, 1.0)
    o_refs = (of_ref, ob_ref)
    zero_blk = jnp.zeros((HEAD_DIM, HEAD_DIM), BF16)
    for g in range(gsz):
        for d in range(N_DIR):
            rows = g if d == 0 else gsz - 1 - g
            for pr in range(n_pair):
                state = s_ref[d * n_pair + pr]
                if g == 0:
                    state = state * keep
                sb = state.astype(BF16)
                s_bd = jnp.concatenate([jnp.concatenate([sb[:, :HEAD_DIM], zero_blk], axis=1),
                                        jnp.concatenate([zero_blk, sb[:, HEAD_DIM:]], axis=1)], axis=0)
                r = _dot(mq_ref[g, d, pr], s_bd)
                bo = bo_ref[g, d, pr]
                s_ref[d * n_pair + pr] = (state * dc_ref[g, d, 0:1, pr * pw:(pr + 1) * pw]
                                          + r[:HEAD_DIM] + bo[:HEAD_DIM])
                o_refs[d][rows * c_:(rows + 1) * c_, pr * pw:(pr + 1) * pw] = (
                    r[HEAD_DIM:] + bo[HEAD_DIM:]).astype(BF16)

    ri = lax.broadcasted_iota(jnp.int32, (c_, w_), 0)
    jj = lax.broadcasted_iota(jnp.int32, (c_, w_), 1) % c_
    eye_w = (ri == jj).astype(F32)
    r2 = lax.broadcasted_iota(jnp.int32, (c_, c_), 0)
    c2 = lax.broadcasted_iota(jnp.int32, (c_, c_), 1)

    def blockdiag_mask(width, lanes_per_head, period=None):
        rh = lax.broadcasted_iota(jnp.int32, (w_, width), 0) // c_
        ln = lax.broadcasted_iota(jnp.int32, (w_, width), 1)
        if period is not None:
            ln = ln % period
        return (rh == ln // lanes_per_head).astype(BF16)

    bd_mask = blockdiag_mask(w_, c_)
    bdk_mask = blockdiag_mask(B_WIDTH, HEAD_DIM)
    bdr_mask = blockdiag_mask(2 * B_WIDTH, HEAD_DIM, period=B_WIDTH)

    dirs = []
    for d in range(N_DIR):
        diff = (ri - jj) if d == 0 else (jj - ri)
        d2 = (r2 - c2) if d == 0 else (c2 - r2)
        dirs.append(dict(incl=diff >= 0, strict=diff > 0, m2=(diff > 0).astype(F32),
                         m1=(d2 >= 0).astype(BF16)))
    srcs = ((qf_ref, kf_ref, vf_ref, gf_ref), (qb_ref, kb_ref, vb_ref, gb_ref))

    def st_scores(t):
        q_ref, k_ref, v_ref, g_ref = srcs[t["d"]]
        pos = t["g"] if t["d"] == 0 else gsz - 1 - t["g"]
        sl = slice(pos * c_, (pos + 1) * c_)
        kc = k_ref[sl, :]
        kq = _dot(jnp.concatenate([kc, q_ref[sl, :]], axis=0), _tile_rows(kc, h_) * bdk_mask, _NT)
        t.update(m=dirs[t["d"]], sl=sl, q_ref=q_ref, k_ref=k_ref, v_ref=v_ref, g_ref=g_ref,
                 kk_w=kq[:c_], qk_w=kq[c_:])

    def st_decay(t):
        m = t["m"]
        gt = t["g_ref"][t["sl"], :]
        gl_w = _lane_blocks([gt[:, h:h + 1] for h in range(h_)], w_, c_)
        d3 = _dot(m["m1"], jnp.concatenate(_split2(gl_w * m["m2"]) + _split2(gt), axis=1))
        dm_w = d3[:, :w_] + d3[:, w_:2 * w_]
        o3 = 2 * w_
        t["gcum"] = d3[:, o3:o3 + HEAD_DIM] + d3[:, o3 + HEAD_DIM:]
        t["gt"] = gt
        t["decay"] = jnp.exp(dm_w)
        beta_w = _lane_blocks([gt[:, h_ + h:h_ + h + 1] for h in range(h_)], w_, c_)
        t["x"] = -jnp.where(m["strict"], t["kk_w"] * beta_w * t["decay"], 0.0)

    def st_inv_first(t):
        xb = t["x"].astype(BF16)
        t["xp"] = _dot(xb, _tile_rows(xb, h_) * bd_mask)
        t["p"] = eye_w + t["x"]

    def st_inv_level(t):
        xpb = t["xp"].astype(BF16)
        r = _dot(jnp.concatenate([t["p"].astype(BF16), xpb], axis=0), _tile_rows(xpb, h_) * bd_mask)
        t["p"] = t["p"] + r[:c_]
        t["xp"] = r[c_:]

    def st_inv_last(t):
        t["p"] = t["p"] + _dot(t["p"].astype(BF16), _tile_rows(t["xp"].astype(BF16), h_) * bd_mask)

    def st_uw(t):
        gt, sl = t["gt"], t["sl"]
        kf = t["k_ref"][sl, :].astype(F32)
        vf = t["v_ref"][sl, :].astype(F32)
        g_hd = _lane_blocks([t["gcum"][:, h:h + 1] for h in range(h_)], B_WIDTH, HEAD_DIM)
        beta_hd = _lane_blocks([gt[:, h_ + h:h_ + h + 1] for h in range(h_)], B_WIDTH, HEAD_DIM)
        gtot = jnp.sum(gt, axis=0, keepdims=True)
        gtot_hd = _lane_blocks([gtot[:, h:h + 1] for h in range(h_)], B_WIDTH, HEAD_DIM)
        t["eg_hd"] = jnp.exp(g_hd)
        t["dc_hd"] = jnp.exp(gtot_hd)
        t["k_tail"] = (kf * jnp.exp(gtot_hd - g_hd)).astype(BF16)
        rhs = jnp.concatenate([vf * beta_hd, kf * (beta_hd * t["eg_hd"])], axis=1).astype(BF16)
        t["uw"] = _dot(t["p"].astype(BF16), _tile_rows(rhs, h_) * bdr_mask)

    def st_intra(t):
        intra = jnp.where(t["m"]["incl"], t["qk_w"] * t["decay"], 0.0).astype(BF16)
        t["iuw"] = _dot(intra, _tile_rows(t["uw"].astype(BF16), h_) * bdr_mask)

    def st_mb(t):
        uw = t["uw"]
        t["mb"] = []
        for h in range(h_):
            hs = slice(h * HEAD_DIM, (h + 1) * HEAD_DIM)
            wu = jnp.concatenate([-uw[:, B_WIDTH + h * HEAD_DIM:B_WIDTH + (h + 1) * HEAD_DIM], uw[:, hs]],
                                 axis=1).astype(BF16)
            t["mb"].append(_dot(t["k_tail"][:, hs], wu, _TN))

    def st_store(t):
        g, d = t["g"], t["d"]
        qf = t["q_ref"][t["sl"], :].astype(F32)
        o_local = t["iuw"][:, :B_WIDTH]
        q_eff = (qf * t["eg_hd"] - t["iuw"][:, B_WIDTH:]).astype(BF16)
        for h in range(h_):
            pr, ls = h // 2, slice((h % 2) * HEAD_DIM, (h % 2 + 1) * HEAD_DIM)
            mb = t["mb"][h]
            mq_ref[g, d, pr, 0:HEAD_DIM, ls] = mb[:, :HEAD_DIM].astype(BF16)
            bo_ref[g, d, pr, 0:HEAD_DIM, ls] = mb[:, HEAD_DIM:]
        for pr in range(n_pair):
            ps = slice(pr * pw, (pr + 1) * pw)
            mq_ref[g, d, pr, HEAD_DIM:, :] = q_eff[:, ps]
            bo_ref[g, d, pr, HEAD_DIM:, :] = o_local[:, ps]
        dc_ref[g, d] = jnp.broadcast_to(t["dc_hd"], (SUBLANES, B_WIDTH))

    stages = ([st_scores, st_decay, st_inv_first] + [st_inv_level] * 4
              + [st_inv_last, st_uw, st_intra, st_mb, st_store])
    inst = [dict(g=g, d=d) for g in range(gsz) for d in range(N_DIR)]
    half = len(inst) // 2
    groups = (inst[:half], inst[half:])
    for k in range(len(stages) + GDN_SKEW):
        for idx in range(half):
            if k < len(stages):
                stages[k](groups[0][idx])
            if 0 <= k - GDN_SKEW < len(stages):
                stages[k - GDN_SKEW](groups[1][idx])


def _gdn(qkvn, gates, gsz):
    b, seq, _ = qkvn.shape
    nb = seq // (gsz * CHUNK)
    rows = gsz * CHUNK
    mrows = HEAD_DIM + CHUNK

    total = b * nb
    n_pair = B_HEADS // 2

    def cur(t):
        return jnp.minimum(t, total - 1)

    def prev(t):
        return jnp.maximum(t - 1, 0)

    def fwd_in(j):
        return lambda t: (cur(t) // nb, cur(t) % nb, j)

    def bwd_in(j):
        return lambda t: (cur(t) // nb, nb - 1 - cur(t) % nb, j)

    out = jax.ShapeDtypeStruct((b, seq, B_WIDTH), BF16)
    return pl.pallas_call(
        functools.partial(_gdn_body, gsz=gsz, nb=nb),
        grid=(total + 1,),
        in_specs=[pl.BlockSpec((None, rows, B_WIDTH), fwd_in(j)) for j in range(3)]
        + [pl.BlockSpec((None, rows, HEAD_DIM), fwd_in(0))]
        + [pl.BlockSpec((None, rows, B_WIDTH), bwd_in(j)) for j in range(3)]
        + [pl.BlockSpec((None, rows, HEAD_DIM), bwd_in(1))],
        out_specs=[
            pl.BlockSpec((None, rows, B_WIDTH), lambda t: (prev(t) // nb, prev(t) % nb, 0)),
            pl.BlockSpec((None, rows, B_WIDTH), lambda t: (prev(t) // nb, nb - 1 - prev(t) % nb, 0)),
        ],
        out_shape=[out, out],
        scratch_shapes=[
            pltpu.VMEM((N_DIR * n_pair, HEAD_DIM, 2 * HEAD_DIM), F32),
            pltpu.VMEM((gsz, N_DIR, n_pair, mrows, 2 * HEAD_DIM), BF16),
            pltpu.VMEM((gsz, N_DIR, n_pair, mrows, 2 * HEAD_DIM), F32),
            pltpu.VMEM((gsz, N_DIR, SUBLANES, B_WIDTH), F32),
        ],
        compiler_params=_params(("arbitrary",)),
        name="gdn",
    )(qkvn, qkvn, qkvn, gates, qkvn, qkvn, qkvn, gates)


def _mem_kv_body(m_ref, nw_ref, w_ref, o_ref):
    x = m_ref[...]
    ms = jnp.mean(x * x, axis=-1, keepdims=True)
    h = (x * lax.rsqrt(ms + EPS) * nw_ref[...]).astype(BF16)
    o_ref[...] = _dot(h, w_ref[...]).astype(BF16)


def _mem_kv(mem2, nw, w, tm):
    rows = mem2.shape[0]
    return pl.pallas_call(
        _mem_kv_body,
        grid=(rows // tm,),
        in_specs=[
            pl.BlockSpec((tm, D_MODEL), lambda i: (i, 0)),
            pl.BlockSpec((1, D_MODEL), lambda i: (0, 0)),
            pl.BlockSpec((D_MODEL, 2 * M_WIDTH), lambda i: (0, 0)),
        ],
        out_specs=pl.BlockSpec((tm, 2 * M_WIDTH), lambda i: (i, 0)),
        out_shape=jax.ShapeDtypeStruct((rows, 2 * M_WIDTH), BF16),
        compiler_params=_params(("parallel",)),
        name="mem_kv",
    )(mem2, nw, w)


def _out_proj_body(ya_ref, of_ref, ob_ref, z_ref, mq_ref, kv_ref, dnw_ref, w_ref, pw_ref, x_ref, o_ref):
    acc = _dot(ya_ref[...], w_ref[0:A_WIDTH, :])
    ob = of_ref[...].astype(F32) + ob_ref[...].astype(F32)
    dnw = dnw_ref[...]
    parts = []
    for h in range(B_HEADS):
        hs = slice(h * HEAD_DIM, (h + 1) * HEAD_DIM)
        oh = ob[:, hs]
        yn = oh * lax.rsqrt(jnp.mean(oh * oh, axis=-1, keepdims=True) + EPS) * dnw
        parts.append((yn * _silu(z_ref[:, hs].astype(F32))).astype(BF16))
    yb = jnp.concatenate(parts, axis=1)
    parts = []
    for h in range(M_HEADS):
        hs = slice(h * HEAD_DIM, (h + 1) * HEAD_DIM)
        s = _dot(mq_ref[:, hs], kv_ref[:, hs], _NT)
        p = jnp.exp(s - jnp.max(s, axis=-1, keepdims=True))
        l = jnp.sum(p, axis=-1, keepdims=True)
        y = _dot(p.astype(BF16), kv_ref[:, M_WIDTH + h * HEAD_DIM:M_WIDTH + (h + 1) * HEAD_DIM]) / l
        zs = slice(B_WIDTH + h * HEAD_DIM, B_WIDTH + (h + 1) * HEAD_DIM)
        parts.append((y * _silu(z_ref[:, zs].astype(F32))).astype(BF16))
    ym = jnp.concatenate(parts, axis=1)
    acc = acc + _dot(yb, w_ref[A_WIDTH:A_WIDTH + B_WIDTH, :])
    acc = acc + _dot(ym, w_ref[A_WIDTH + B_WIDTH:, :])
    y = acc * lax.rsqrt(jnp.mean(acc * acc, axis=-1, keepdims=True) + EPS) * pw_ref[...]
    o_ref[...] = x_ref[...] + y


def _out_proj(ya, o_f, o_b, z, mq, mkv, dnw, w, pw, x, tm):
    b, seq, _ = ya.shape
    n_mem = mkv.shape[1]
    zw = B_WIDTH + M_WIDTH
    assert A_WIDTH % zw == 0
    row = lambda bi, i: (bi, i, 0)
    const = lambda bi, i: (0, 0)
    return pl.pallas_call(
        _out_proj_body,
        grid=(b, seq // tm),
        in_specs=[
            pl.BlockSpec((None, tm, A_WIDTH), row),
            pl.BlockSpec((None, tm, B_WIDTH), row),
            pl.BlockSpec((None, tm, B_WIDTH), row),
            pl.BlockSpec((None, tm, zw), lambda bi, i: (bi, i, A_WIDTH // zw)),
            pl.BlockSpec((None, tm, M_WIDTH), row),
            pl.BlockSpec((None, n_mem, 2 * M_WIDTH), lambda bi, i: (bi, 0, 0)),
            pl.BlockSpec((1, HEAD_DIM), const),
            pl.BlockSpec((D_MIX, D_MODEL), const),
            pl.BlockSpec((1, D_MODEL), const),
            pl.BlockSpec((None, tm, D_MODEL), row),
        ],
        out_specs=pl.BlockSpec((None, tm, D_MODEL), row),
        out_shape=jax.ShapeDtypeStruct((b, seq, D_MODEL), F32),
        compiler_params=_params(("parallel", "parallel")),
        name="out_proj",
    )(ya, o_f, o_b, z, mq, mkv, dnw, w, pw, x)


def _rope_tables(seq):
    t = np.arange(seq)
    inv_freq = (np.float32(ROPE_THETA) ** (-np.arange(0, ROPE_AXIS_DIM, 2, dtype=np.float32) / ROPE_AXIS_DIM)
                ).astype(np.float32)
    ang_row = (t // GRID_W).astype(np.float32)[:, None] * inv_freq
    ang_col = (t % GRID_W).astype(np.float32)[:, None] * inv_freq
    cr, sr, cc, sc = np.cos(ang_row), np.sin(ang_row), np.cos(ang_col), np.sin(ang_col)
    return (jnp.asarray(np.concatenate([cr, cr, cc, cc], axis=-1), F32),
            jnp.asarray(np.concatenate([-sr, sr, -sc, sc], axis=-1), F32))


def _arrange_w_tail(w):
    pts = np.cumsum([N_DIR * B_HEADS, N_DIR * B_HEADS, M_WIDTH])
    ba, bb, mq, z = jnp.split(w[:, HEAD_COLS:], pts, axis=1)
    gate_blocks = []
    for d in range(N_DIR):
        sl = slice(d * B_HEADS, (d + 1) * B_HEADS)
        pad = jnp.zeros((w.shape[0], HEAD_DIM - 2 * B_HEADS), w.dtype)
        gate_blocks += [ba[:, sl], bb[:, sl], pad]
    return jnp.concatenate(gate_blocks + [mq, z], axis=1)


def _gate_lane_vec(p):
    out = jnp.zeros((N_DIR, HEAD_DIM), F32).at[:, :B_HEADS].set(p.astype(F32))
    return out.reshape(1, GATE_WIDTH)


def kernel(x, mem, norm_pre_w, w_in, q_norm_w, k_norm_w, conv_w, a_log, dt_bias, delta_norm_w,
           mem_norm_w, w_mem_kv, w_out, norm_post_w):
    b, seq, _ = x.shape
    n_mem = mem.shape[1]
    assert x.shape[2] == D_MODEL and w_in.shape == (1, D_MODEL, HEAD_COLS + 2 * N_DIR * B_HEADS + M_WIDTH + D_MIX)
    assert seq % SEQ_TILE == 0 and (seq // SEQ_TILE) % 2 == 0 and seq % GRID_W == 0
    assert seq % (GDN_CHUNKS_PER_STEP * CHUNK) == 0 and (b * n_mem) % MEM_ROW_TILE == 0
    l = 0
    tm = SEQ_TILE

    cos_t, sin_t = _rope_tables(seq)
    qkw = jnp.zeros((SUBLANES, HEAD_DIM), F32).at[0].set(q_norm_w[l]).at[1].set(k_norm_w[l])
    cw = jnp.zeros((SUBLANES, B_QKV_WIDTH), F32).at[:CONV_K].set(conv_w[l])
    w_bf = w_in[l].astype(BF16)
    aq, ak, av, qkvn, gates, mq, z = _in_proj(
        x.reshape(b * seq, D_MODEL), norm_pre_w[l].reshape(1, D_MODEL), w_bf, _arrange_w_tail(w_bf),
        qkw, cos_t, sin_t, cw, _gate_lane_vec(a_log[l]), _gate_lane_vec(dt_bias[l]), seq, tm)
    shp = lambda a: a.reshape(b, seq, a.shape[-1])
    ak, qkvn, gates, mq, z = map(shp, (ak, qkvn, gates, mq, z))

    ya = _attention(aq, ak, av, z)
    o_f, o_b = _gdn(qkvn, gates, gsz=GDN_CHUNKS_PER_STEP)

    mkv = _mem_kv(mem.reshape(b * n_mem, D_MODEL), mem_norm_w[l].reshape(1, D_MODEL),
                  w_mem_kv[l].astype(BF16), tm=MEM_ROW_TILE)

    return _out_proj(ya, o_f, o_b, z, mq, mkv.reshape(b, n_mem, 2 * M_WIDTH),
                     delta_norm_w[l].reshape(1, HEAD_DIM), w_out[l].astype(BF16),
                     norm_post_w[l].reshape(1, D_MODEL), x, tm=SEQ_TILE)
```
